```python
import jax, jax.numpy as jnp
from jax import lax
import numpy as np

D_MODEL = 1024
BATCH = 8
SEQ = 4096
DEPTH = 1

N_META = 16
GRID_W = 64
MLA_HEADS = 8
MLA_NOPE = 64
MLA_ROPE = 32
MLA_V = 64
MLA_QK = MLA_NOPE + MLA_ROPE
Q_LORA = 384
KV_LORA = 256
NA_HEADS = 8
NA_HEAD_DIM = 64
NA_MAX_ROWS = 8
NA_KW = 16
D_FF = 2816
Q_BLOCK = 128
ROPE_THETA = 10000.0
EPS = 1e-6
MLA_WIDTH = MLA_HEADS * MLA_V
NA_WIDTH = NA_HEADS * NA_HEAD_DIM
IN_SPLITS = (Q_LORA, KV_LORA, MLA_ROPE, NA_WIDTH, NA_WIDTH, NA_WIDTH, D_MODEL, D_MODEL)
IN_WIDTH = Q_LORA + KV_LORA + MLA_ROPE + 3 * NA_WIDTH + 2 * D_MODEL

kernel_name = 'hybrid_mla_natten_macaron_block'


def rmsnorm(x, g):
    xf = x.astype(jnp.float32)
    y = xf * lax.rsqrt(jnp.mean(xf * xf, axis=-1, keepdims=True) + EPS)
    return (y * g.astype(jnp.float32)).astype(x.dtype)


def swiglu(x, w_gate, w_up, w_down):
    return (jax.nn.silu(x @ w_gate) * (x @ w_up)) @ w_down


def rope_tables(t_len):
    half = MLA_ROPE // 2
    pos = jnp.arange(t_len, dtype=jnp.float32)
    inv = ROPE_THETA ** (-jnp.arange(half, dtype=jnp.float32) / half)
    ang = pos[:, None] * inv[None, :]
    return jnp.cos(ang), jnp.sin(ang)


def apply_rope(x, cos, sin):
    half = MLA_ROPE // 2
    c = cos[None, :, None, :].astype(x.dtype)
    s = sin[None, :, None, :].astype(x.dtype)
    x1, x2 = x[..., :half], x[..., half:]
    return jnp.concatenate([x1 * c - x2 * s, x1 * s + x2 * c], axis=-1)


def dense_attend(q, k, v, scale):
    s = jnp.einsum('bqhd,bkhd->bhqk', q, k).astype(jnp.float32) * scale
    p = jax.nn.softmax(s, axis=-1).astype(v.dtype)
    return jnp.einsum('bhqk,bkhd->bqhd', p, v)


def mla_mixer(c_q, c_kv, k_r, q_a_norm, w_uq, kv_a_norm, w_ukv, q_head_norm, k_head_norm):
    b, t_len, _ = c_q.shape
    q = (rmsnorm(c_q, q_a_norm) @ w_uq).reshape(b, t_len, MLA_HEADS, MLA_QK)
    kv = (rmsnorm(c_kv, kv_a_norm) @ w_ukv).reshape(b, t_len, MLA_HEADS, MLA_NOPE + MLA_V)
    k_nope, v = kv[..., :MLA_NOPE], kv[..., MLA_NOPE:]
    k_rope = jnp.broadcast_to(k_r[:, :, None, :], (b, t_len, MLA_HEADS, MLA_ROPE))
    k = jnp.concatenate([k_nope, k_rope], axis=-1)
    q = rmsnorm(q, q_head_norm)
    k = rmsnorm(k, k_head_norm)
    cos, sin = rope_tables(t_len)
    q = jnp.concatenate([q[..., :MLA_NOPE], apply_rope(q[..., MLA_NOPE:], cos, sin)], axis=-1)
    k = jnp.concatenate([k[..., :MLA_NOPE], apply_rope(k[..., MLA_NOPE:], cos, sin)], axis=-1)
    scale = MLA_QK ** -0.5
    n_real = t_len - N_META
    o_meta = dense_attend(q[:, :N_META], k, v, scale).reshape(b, N_META, MLA_WIDTH)
    q_blocks = jnp.moveaxis(q[:, N_META:].reshape(b, n_real // Q_BLOCK, Q_BLOCK, MLA_HEADS, MLA_QK), 1, 0)
    o_real = lax.map(lambda qb: dense_attend(qb, k, v, scale), q_blocks)
    o_real = jnp.moveaxis(o_real, 0, 1).reshape(b, n_real, MLA_WIDTH)
    return jnp.concatenate([o_meta, o_real], axis=1)


def na_mixer(q, k, v, q_head_norm, k_head_norm, rel_bias):
    b, t_len, _ = q.shape
    q = rmsnorm(q.reshape(b, t_len, NA_HEADS, NA_HEAD_DIM), q_head_norm)
    k = rmsnorm(k.reshape(b, t_len, NA_HEADS, NA_HEAD_DIM), k_head_norm)
    v = v.reshape(b, t_len, NA_HEADS, NA_HEAD_DIM)
    scale = NA_HEAD_DIM ** -0.5
    n_real = t_len - N_META
    rows = n_real // GRID_W
    kh = min(NA_MAX_ROWS, rows)
    q_m, k_m, v_m = q[:, :N_META], k[:, :N_META], v[:, :N_META]
    qg = q[:, N_META:].reshape(b, rows, GRID_W, NA_HEADS, NA_HEAD_DIM)
    kg = k[:, N_META:].reshape(b, rows, GRID_W, NA_HEADS, NA_HEAD_DIM)
    vg = v[:, N_META:].reshape(b, rows, GRID_W, NA_HEADS, NA_HEAD_DIM)
    col_start = np.clip(np.arange(GRID_W) - NA_KW // 2, 0, GRID_W - NA_KW)
    col_idx = col_start[:, None] + np.arange(NA_KW)[None, :]
    dc = col_idx - np.arange(GRID_W)[:, None] + (NA_KW - 1)
    bias_cols = rel_bias[:, :, dc]
    o_meta = dense_attend(q_m, k_m, v_m, scale).reshape(b, N_META, NA_WIDTH)

    def row_block(r):
        rs = jnp.clip(r - kh // 2, 0, rows - kh)
        q_row = lax.dynamic_index_in_dim(qg, r, axis=1, keepdims=False)
        k_win = lax.dynamic_slice_in_dim(kg, rs, kh, axis=1)[:, :, col_idx]
        v_win = lax.dynamic_slice_in_dim(vg, rs, kh, axis=1)[:, :, col_idx]
        dr = rs + jnp.arange(kh) - r + (NA_MAX_ROWS - 1)
        bias = jnp.transpose(bias_cols[:, dr], (0, 2, 1, 3)).reshape(NA_HEADS, GRID_W, kh * NA_KW)
        s_win = jnp.einsum('bchd,bicjhd->bhcij', q_row, k_win).reshape(b, NA_HEADS, GRID_W, kh * NA_KW)
        s_win = s_win.astype(jnp.float32) * scale + bias.astype(jnp.float32)[None]
        s_meta = jnp.einsum('bchd,bmhd->bhcm', q_row, k_m).astype(jnp.float32) * scale
        p = jax.nn.softmax(jnp.concatenate([s_meta, s_win], axis=-1), axis=-1).astype(v.dtype)
        p_meta = p[..., :N_META]
        p_win = p[..., N_META:].reshape(b, NA_HEADS, GRID_W, kh, NA_KW)
        return (jnp.einsum('bhcm,bmhd->bchd', p_meta, v_m)
                + jnp.einsum('bhcij,bicjhd->bchd', p_win, v_win))

    o_real = lax.map(row_block, jnp.arange(rows))
    o_real = jnp.moveaxis(o_real, 0, 1).reshape(b, n_real, NA_WIDTH)
    return jnp.concatenate([o_meta, o_real], axis=1)


def setup_inputs(seed: int = 0) -> dict:
    key = jax.random.key(seed)
    ks = jax.random.split(key, 24)

    def dense(k, shape, fan_in):
        return jax.random.normal(k, shape, jnp.float32) * fan_in ** -0.5

    def gain(k, shape):
        return 1.0 + 0.02 * jax.random.normal(k, shape, jnp.float32)

    L = DEPTH
    return {
        'x': jax.random.normal(ks[0], (BATCH, SEQ, D_MODEL), jnp.float32),
        'meta_tokens': jax.random.normal(ks[1], (N_META, D_MODEL), jnp.float32),
        'ffn1_norm': gain(ks[2], (L, D_MODEL)),
        'ffn1_w_gate': dense(ks[3], (L, D_MODEL, D_FF), D_MODEL),
        'ffn1_w_up': dense(ks[4], (L, D_MODEL, D_FF), D_MODEL),
        'ffn1_w_down': dense(ks[5], (L, D_FF, D_MODEL), D_FF),
        'mix_norm': gain(ks[6], (L, D_MODEL)),
        'w_in': dense(ks[7], (L, D_MODEL, IN_WIDTH), D_MODEL),
        'mla_q_a_norm': gain(ks[8], (L, Q_LORA)),
        'mla_w_uq': dense(ks[9], (L, Q_LORA, MLA_HEADS * MLA_QK), Q_LORA),
        'mla_kv_a_norm': gain(ks[10], (L, KV_LORA)),
        'mla_w_ukv': dense(ks[11], (L, KV_LORA, MLA_HEADS * (MLA_NOPE + MLA_V)), KV_LORA),
        'mla_q_head_norm': gain(ks[12], (L, MLA_QK)),
        'mla_k_head_norm': gain(ks[13], (L, MLA_QK)),
        'na_q_head_norm': gain(ks[14], (L, NA_HEAD_DIM)),
        'na_k_head_norm': gain(ks[15], (L, NA_HEAD_DIM)),
        'na_rel_bias': 0.02 * jax.random.normal(ks[16], (L, NA_HEADS, 2 * NA_MAX_ROWS - 1, 2 * NA_KW - 1), jnp.float32),
        'w_branch_a': dense(ks[17], (L, MLA_WIDTH, D_MODEL), MLA_WIDTH),
        'w_branch_b': dense(ks[18], (L, NA_WIDTH, D_MODEL), NA_WIDTH),
        'w_out': dense(ks[19], (L, D_MODEL, D_MODEL), D_MODEL),
        'ffn2_norm': gain(ks[20], (L, D_MODEL)),
        'ffn2_w_gate': dense(ks[21], (L, D_MODEL, D_FF), D_MODEL),
        'ffn2_w_up': dense(ks[22], (L, D_MODEL, D_FF), D_MODEL),
        'ffn2_w_down': dense(ks[23], (L, D_FF, D_MODEL), D_FF),
    }


def reference(x, meta_tokens, ffn1_norm, ffn1_w_gate, ffn1_w_up, ffn1_w_down, mix_norm, w_in,
              mla_q_a_norm, mla_w_uq, mla_kv_a_norm, mla_w_ukv, mla_q_head_norm, mla_k_head_norm,
              na_q_head_norm, na_k_head_norm, na_rel_bias, w_branch_a, w_branch_b, w_out,
              ffn2_norm, ffn2_w_gate, ffn2_w_up, ffn2_w_down):
    b = x.shape[0]
    split_points = np.cumsum(IN_SPLITS)[:-1].tolist()
    meta = jnp.broadcast_to(meta_tokens.astype(x.dtype)[None], (b, N_META, D_MODEL))
    h = jnp.concatenate([meta, x], axis=1)
    for l in range(DEPTH):
        h = h + 0.5 * swiglu(rmsnorm(h, ffn1_norm[l]), ffn1_w_gate[l], ffn1_w_up[l], ffn1_w_down[l])
        xn = rmsnorm(h, mix_norm[l])
        proj = xn @ w_in[l]
        c_q, c_kv, k_r, q_na, k_na, v_na, g_a, g_b = jnp.split(proj, split_points, axis=-1)
        o_a = mla_mixer(c_q, c_kv, k_r, mla_q_a_norm[l], mla_w_uq[l], mla_kv_a_norm[l], mla_w_ukv[l],
                        mla_q_head_norm[l], mla_k_head_norm[l])
        o_b = na_mixer(q_na, k_na, v_na, na_q_head_norm[l], na_k_head_norm[l], na_rel_bias[l])
        merged = jax.nn.sigmoid(g_a) * (o_a @ w_branch_a[l]) + jax.nn.sigmoid(g_b) * (o_b @ w_branch_b[l])
        h = h + merged @ w_out[l]
        h = h + 0.5 * swiglu(rmsnorm(h, ffn2_norm[l]), ffn2_w_gate[l], ffn2_w_up[l], ffn2_w_down[l])
    return h[:, N_META:]
```

```python
import functools

import jax
import jax.numpy as jnp
import numpy as np
from jax import lax
from jax.experimental import pallas as pl
from jax.experimental.pallas import tpu as pltpu

D_MODEL = 1024
SEQ = 4096
N_META = 16
GRID_W = 64
HEADS = 8
MLA_NOPE = 64
MLA_ROPE = 32
MLA_QK = MLA_NOPE + MLA_ROPE
Q_LORA = 384
KV_LORA = 256
NA_HEAD_DIM = 64
NA_MAX_ROWS = 8
NA_KW = 16
D_FF = 2816
ROPE_THETA = 10000.0
EPS = 1e-6
HALF_WIDTH = 512

LANES = 128
HEAD_PAIR = 2 * LANES
FF_CHUNK = 256
N_FF_CHUNKS = D_FF // FF_CHUNK
assert N_FF_CHUNKS * FF_CHUNK == D_FF

TOKEN_TILE = 512
MLA_Q_TILE = 256
NA_GROUP_ROWS = 4
NA_GROUP = NA_GROUP_ROWS * GRID_W
NA_SLAB_ROWS = NA_GROUP_ROWS + NA_MAX_ROWS
NA_SLAB = NA_SLAB_ROWS * GRID_W
NA_ROWS = SEQ // GRID_W
NA_GROUPS = NA_ROWS // NA_GROUP_ROWS
MASKED = -1e30

_O_CQ, _O_CKV, _O_KR = 0, Q_LORA, Q_LORA + KV_LORA
_O_QNA = _O_KR + LANES
_O_KNA = _O_QNA + HALF_WIDTH
_O_VNA = _O_KNA + HALF_WIDTH
_IN_R_WIDTH = _O_VNA + HALF_WIDTH

_BF = jnp.bfloat16
_F32 = jnp.float32


def _dot(a, b):
    return jnp.dot(a, b, preferred_element_type=_F32)


def _dot_nt(a, b):
    return lax.dot_general(a, b, (((1,), (1,)), ((), ())), preferred_element_type=_F32)


def _sigmoid(x):
    return 1.0 / (1.0 + jnp.exp(-x))


def _rms(x, g):
    return x * lax.rsqrt(jnp.mean(x * x, axis=-1, keepdims=True) + EPS) * g


def _swiglu_into(acc_ref, xn, wg_ref, wu_ref, wd_ref):
    acc_ref[...] = jnp.zeros_like(acc_ref)

    def chunk(c, carry):
        g = _dot(xn, wg_ref[c])
        u = _dot(xn, wu_ref[c])
        a = (g * _sigmoid(g) * u).astype(_BF)
        acc_ref[...] += _dot(a, wd_ref[c])
        return carry

    lax.fori_loop(0, N_FF_CHUNKS, chunk, 0)


def _spread_heads(x):
    lane = lax.broadcasted_iota(jnp.int32, (x.shape[0], LANES), 1)
    out = []
    for h in range(HEADS):
        blk = x[:, LANES * (h // 2):LANES * (h // 2 + 1)]
        if h % 2:
            blk = pltpu.roll(blk, 64, 1)
        out.append(jnp.where(lane < 64, blk, 0.0))
    return jnp.concatenate(out, axis=1)


def _rope(x, c, sa, sb):
    return x * c + pltpu.roll(x, 16, 1) * sa + pltpu.roll(x, LANES - 16, 1) * sb


def _ffn1_proj_kernel(x_ref, g1_ref, wg_ref, wu_ref, wd_ref, gmix_ref, win_ref, qag_ref, wuq_ref,
                      kvag_ref, wukv_ref, gq_ref, gkn_ref, gkr_ref, gqna_ref, gkna_ref,
                      rc_ref, rsa_ref, rsb_ref,
                      h1_ref, qm_ref, km_ref, vm_ref, qn_ref, kn_ref, vn_ref, acc_ref):
    x = x_ref[...]
    _swiglu_into(acc_ref, _rms(x, g1_ref[...]).astype(_BF), wg_ref, wu_ref, wd_ref)
    h1 = x + 0.5 * acc_ref[...]
    h1_ref[...] = h1
    xn = _rms(h1, gmix_ref[...]).astype(_BF)

    rows = x.shape[0]
    lane = lax.broadcasted_iota(jnp.int32, (rows, LANES), 1)
    ones_hi = jnp.where(lane >= 64, 1.0, 0.0)
    rc, rsa, rsb = rc_ref[...], rsa_ref[...], rsb_ref[...]

    cq = _dot(xn, win_ref[:, _O_CQ:_O_CQ + Q_LORA])
    q = _dot(_rms(cq, qag_ref[...]).astype(_BF), wuq_ref[...])
    gq = gq_ref[...]
    for h in range(HEADS):
        qh = q[:, LANES * h:LANES * (h + 1)]
        r = lax.rsqrt(jnp.sum(qh * qh, axis=-1, keepdims=True) * (1.0 / MLA_QK) + EPS)
        qm_ref[:, LANES * h:LANES * (h + 1)] = _rope(qh * r * gq, rc, rsa, rsb).astype(_BF)

    ckv = _dot(xn, win_ref[:, _O_CKV:_O_CKV + KV_LORA])
    kvu = _dot(_rms(ckv, kvag_ref[...]).astype(_BF), wukv_ref[...])
    kr = _dot(xn, win_ref[:, _O_KR:_O_KR + LANES])
    ss_r = jnp.sum(kr * kr, axis=-1, keepdims=True)
    krr = _rope(kr * gkr_ref[...], rc, rsa, rsb)
    gkn = gkn_ref[...]
    for h in range(HEADS):
        kh = kvu[:, LANES * h:LANES * (h + 1)]
        r = lax.rsqrt((jnp.sum(kh * kh, axis=-1, keepdims=True) + ss_r) * (1.0 / MLA_QK) + EPS)
        km_ref[:, LANES * h:LANES * (h + 1)] = ((kh * gkn + krr) * r).astype(_BF)
    v = _spread_heads(kvu[:, HEADS * LANES:])
    for h in range(HEADS):
        vm_ref[:, LANES * h:LANES * (h + 1)] = (v[:, LANES * h:LANES * (h + 1)] + ones_hi).astype(_BF)

    for off, g_ref, o_ref in ((_O_QNA, gqna_ref, qn_ref), (_O_KNA, gkna_ref, kn_ref)):
        t = _spread_heads(_dot(xn, win_ref[:, off:off + HALF_WIDTH]))
        g = g_ref[...]
        for h in range(HEADS):
            th = t[:, LANES * h:LANES * (h + 1)]
            r = lax.rsqrt(jnp.sum(th * th, axis=-1, keepdims=True) * (1.0 / NA_HEAD_DIM) + EPS)
            o_ref[:, LANES * h:LANES * (h + 1)] = (th * r * g).astype(_BF)
    t = _spread_heads(_dot(xn, win_ref[:, _O_VNA:_O_VNA + HALF_WIDTH]))
    for h in range(HEADS):
        vn_ref[:, LANES * h:LANES * (h + 1)] = (t[:, LANES * h:LANES * (h + 1)] + ones_hi).astype(_BF)


def _const_spec(shape):
    nd = len(shape)
    return pl.BlockSpec(shape, lambda i: (0,) * nd, pipeline_mode=pl.Buffered(1))


def _ffn1_proj(x2d, rope_tabs, p, tile, vmem_mb):
    n = x2d.shape[0]
    tabs_per_seq = rope_tabs[0].shape[0] // tile
    row_spec = lambda w: pl.BlockSpec((tile, w), lambda i: (i, 0))
    tab_spec = pl.BlockSpec((tile, LANES), lambda i: (i % tabs_per_seq, 0))
    consts = [p["g1"], p["wg1"], p["wu1"], p["wd1"], p["gmix"], p["win_r"], p["qag"], p["wuq"],
              p["kvag"], p["wukv"], p["gq"], p["gkn"], p["gkr"], p["gqna"], p["gkna"]]
    wide = HEADS * LANES
    out_shape = [jax.ShapeDtypeStruct((n, D_MODEL), _F32)] + [jax.ShapeDtypeStruct((n, wide), _BF)] * 6
    return pl.pallas_call(
        _ffn1_proj_kernel,
        grid=(n // tile,),
        in_specs=[row_spec(D_MODEL)] + [_const_spec(c.shape) for c in consts] + [tab_spec] * 3,
        out_specs=[row_spec(D_MODEL)] + [row_spec(wide)] * 6,
        out_shape=out_shape,
        scratch_shapes=[pltpu.VMEM((tile, D_MODEL), _F32)],
        compiler_params=pltpu.CompilerParams(dimension_semantics=("arbitrary",),
                                             vmem_limit_bytes=vmem_mb * 2 ** 20),
        name="ffn1_proj",
    )(x2d, *consts, *rope_tabs)


def _softmax_pv(s, s_meta, v, v_meta):
    m = jnp.maximum(jnp.max(s, axis=-1, keepdims=True), jnp.max(s_meta, axis=-1, keepdims=True))
    acc = _dot(jnp.exp(s - m).astype(_BF), v) + _dot(jnp.exp(s_meta - m).astype(_BF), v_meta)
    return acc / acc[:, 64:65]


def _pair_lanes(o0, o1):
    lane = lax.broadcasted_iota(jnp.int32, o0.shape, 1)
    return jnp.where(lane < 64, o0, pltpu.roll(o1, 64, 1))


def _mla_kernel(q_ref, k_ref, v_ref, km_ref, vm_ref, o_ref):
    outs = []
    for hh in range(2):
        sl = slice(LANES * hh, LANES * (hh + 1))
        q = q_ref[0, :, sl]
        outs.append(_softmax_pv(_dot_nt(q, k_ref[0, :, sl]), _dot_nt(q, km_ref[:, sl]),
                                v_ref[0, :, sl], vm_ref[:, sl]))
    o_ref[0] = _pair_lanes(*outs).astype(_BF)


def _mla_attn(q, k, v, k_meta, v_meta):
    b = q.shape[0]
    kv_spec = pl.BlockSpec((1, SEQ, HEAD_PAIR), lambda bi, hp, i: (bi, 0, hp))
    meta_spec = pl.BlockSpec((N_META, HEAD_PAIR), lambda bi, hp, i: (0, hp))
    return pl.pallas_call(
        _mla_kernel,
        grid=(b, HEADS // 2, SEQ // MLA_Q_TILE),
        in_specs=[pl.BlockSpec((1, MLA_Q_TILE, HEAD_PAIR), lambda bi, hp, i: (bi, i, hp)),
                  kv_spec, kv_spec, meta_spec, meta_spec],
        out_specs=pl.BlockSpec((1, MLA_Q_TILE, LANES), lambda bi, hp, i: (bi, i, hp)),
        out_shape=jax.ShapeDtypeStruct((b, SEQ, HALF_WIDTH), _BF),
        compiler_params=pltpu.CompilerParams(dimension_semantics=("arbitrary",) * 3,
                                             vmem_limit_bytes=48 * 2 ** 20),
        name="mla_attn",
    )(q, k, v, k_meta, v_meta)


def _na_kernel(q_ref, k_ref, v_ref, km_ref, vm_ref, tab_ref, o_ref):
    def group(g, carry):
        slab = pl.multiple_of(jnp.clip(g * NA_GROUP_ROWS - NA_MAX_ROWS // 2, 0, NA_ROWS - NA_SLAB_ROWS) * GRID_W,
                              GRID_W)
        q0 = pl.multiple_of(g * NA_GROUP, NA_GROUP)
        variant = jnp.where(g == 0, 0, jnp.where(g == NA_GROUPS - 1, 2, 1))
        outs = []
        for hh in range(2):
            sl = slice(LANES * hh, LANES * (hh + 1))
            q = q_ref[0, pl.ds(q0, NA_GROUP), sl]
            s = _dot_nt(q, k_ref[0, pl.ds(slab, NA_SLAB), sl]) + tab_ref[hh, variant]
            outs.append(_softmax_pv(s, _dot_nt(q, km_ref[:, sl]), v_ref[0, pl.ds(slab, NA_SLAB), sl], vm_ref[:, sl]))
        o_ref[0, pl.ds(q0, NA_GROUP), :] = _pair_lanes(*outs).astype(_BF)
        return carry

    lax.fori_loop(0, NA_GROUPS, group, 0)


def _na_attn(q, k, v, k_meta, v_meta, table):
    b = q.shape[0]
    seq_spec = pl.BlockSpec((1, SEQ, HEAD_PAIR), lambda hp, bi: (bi, 0, hp))
    meta_spec = pl.BlockSpec((N_META, HEAD_PAIR), lambda hp, bi: (0, hp))
    return pl.pallas_call(
        _na_kernel,
        grid=(HEADS // 2, b),
        in_specs=[seq_spec, seq_spec, seq_spec, meta_spec, meta_spec,
                  pl.BlockSpec((2, 3, NA_GROUP, NA_SLAB), lambda hp, bi: (hp, 0, 0, 0))],
        out_specs=pl.BlockSpec((1, SEQ, LANES), lambda hp, bi: (bi, 0, hp)),
        out_shape=jax.ShapeDtypeStruct((b, SEQ, HALF_WIDTH), _BF),
        compiler_params=pltpu.CompilerParams(dimension_semantics=("arbitrary",) * 2,
                                             vmem_limit_bytes=48 * 2 ** 20),
        name="na_attn",
    )(q, k, v, k_meta, v_meta, table)


def _na_bias_table(rel_bias):
    i = np.arange(NA_GROUP_ROWS)[:, None, None, None]
    c = np.arange(GRID_W)[None, :, None, None]
    j = np.arange(NA_SLAB_ROWS)[None, None, :, None]
    c2 = np.arange(GRID_W)[None, None, None, :]
    cs = np.clip(c - NA_KW // 2, 0, GRID_W - NA_KW)
    tabs = []
    for g in (0, 1, NA_GROUPS - 1):
        r = g * NA_GROUP_ROWS + i
        rs = np.clip(r - NA_MAX_ROWS // 2, 0, NA_ROWS - NA_MAX_ROWS)
        kr = np.clip(g * NA_GROUP_ROWS - NA_MAX_ROWS // 2, 0, NA_ROWS - NA_SLAB_ROWS) + j
        valid = (kr >= rs) & (kr < rs + NA_MAX_ROWS) & (c2 >= cs) & (c2 < cs + NA_KW)
        dr = np.clip(kr - r + NA_MAX_ROWS - 1, 0, 2 * NA_MAX_ROWS - 2)
        dc = np.clip(c2 - c + NA_KW - 1, 0, 2 * NA_KW - 2)
        shape = (NA_GROUP_ROWS, GRID_W, NA_SLAB_ROWS, GRID_W)
        dr, dc, valid = (np.broadcast_to(a, shape).reshape(NA_GROUP, NA_SLAB) for a in (dr, dc, valid))
        tabs.append(jnp.where(valid[None], rel_bias[:, dr, dc], MASKED))
    return jnp.stack(tabs, axis=1)


def _merge_ffn2_kernel(h1_ref, oa_ref, ob_ref, gmix_ref, wga_ref, wgb_ref, wba_ref, wbb_ref, wout_ref,
                       g2_ref, wg_ref, wu_ref, wd_ref, out_ref, acc_ref):
    h1 = h1_ref[...]
    xn = _rms(h1, gmix_ref[...]).astype(_BF)
    merged = (_sigmoid(_dot(xn, wga_ref[...])) * _dot(oa_ref[...], wba_ref[...])
              + _sigmoid(_dot(xn, wgb_ref[...])) * _dot(ob_ref[...], wbb_ref[...]))
    h2 = h1 + _dot(merged.astype(_BF), wout_ref[...])
    _swiglu_into(acc_ref, _rms(h2, g2_ref[...]).astype(_BF), wg_ref, wu_ref, wd_ref)
    out_ref[...] = h2 + 0.5 * acc_ref[...]


def _merge_ffn2(h1, oa, ob, p):
    n = h1.shape[0]
    row_spec = lambda w: pl.BlockSpec((TOKEN_TILE, w), lambda i: (i, 0))
    consts = [p["gmix"], p["wga"], p["wgb"], p["wba"], p["wbb"], p["wout"], p["g2"], p["wg2"], p["wu2"], p["wd2"]]
    return pl.pallas_call(
        _merge_ffn2_kernel,
        grid=(n // TOKEN_TILE,),
        in_specs=[row_spec(D_MODEL), row_spec(HALF_WIDTH), row_spec(HALF_WIDTH)]
                 + [_const_spec(c.shape) for c in consts],
        out_specs=row_spec(D_MODEL),
        out_shape=jax.ShapeDtypeStruct((n, D_MODEL), _F32),
        scratch_shapes=[pltpu.VMEM((TOKEN_TILE, D_MODEL), _F32)],
        compiler_params=pltpu.CompilerParams(dimension_semantics=("arbitrary",),
                                             vmem_limit_bytes=56 * 2 ** 20),
        name="merge_ffn2",
    )(h1, oa, ob, *consts)


def _pad_heads(w, width):
    lead = w.shape[:-1]
    w = w.reshape(lead + (HEADS, width))
    return jnp.pad(w, [(0, 0)] * len(lead) + [(0, 0), (0, LANES - width)]).reshape(lead + (HEADS * LANES,))


def _lane_row(v, offset):
    return jnp.zeros((1, LANES), _F32).at[0, offset:offset + v.shape[0]].set(v)


def _ff_chunks(w_gate, w_up, w_down):
    cols = lambda w: w.reshape(D_MODEL, N_FF_CHUNKS, FF_CHUNK).transpose(1, 0, 2).astype(_BF)
    return cols(w_gate), cols(w_up), w_down.reshape(N_FF_CHUNKS, FF_CHUNK, D_MODEL).astype(_BF)


def _rope_tables():
    half = MLA_ROPE // 2
    pos = jnp.arange(N_META + SEQ, dtype=_F32)
    inv = ROPE_THETA ** (-jnp.arange(half, dtype=_F32) / half)
    ang = pos[:, None] * inv[None, :]
    cos, sin = jnp.cos(ang), jnp.sin(ang)
    n = pos.shape[0]
    zeros = jnp.zeros((n, half), _F32)
    c = jnp.concatenate([jnp.ones((n, MLA_NOPE), _F32), cos, cos, jnp.zeros((n, LANES - MLA_QK), _F32)], axis=1)
    lo = jnp.zeros((n, MLA_NOPE), _F32)
    hi = jnp.zeros((n, LANES - MLA_QK), _F32)
    sa = jnp.concatenate([lo, zeros, sin, hi], axis=1)
    sb = jnp.concatenate([lo, -sin, zeros, hi], axis=1)
    return c, sa, sb


def kernel(x, meta_tokens, ffn1_norm, ffn1_w_gate, ffn1_w_up, ffn1_w_down, mix_norm, w_in, mla_q_a_norm, mla_w_uq,
           mla_kv_a_norm, mla_w_ukv, mla_q_head_norm, mla_k_head_norm, na_q_head_norm, na_k_head_norm, na_rel_bias,
           w_branch_a, w_branch_b, w_out, ffn2_norm, ffn2_w_gate, ffn2_w_up, ffn2_w_down):
    assert ffn1_norm.shape[0] == 1, "single-layer block"
    b = x.shape[0]
    row = lambda v: v.reshape(1, -1).astype(_F32)
    wi = w_in[0]
    splits = np.cumsum((Q_LORA, KV_LORA, MLA_ROPE, HALF_WIDTH, HALF_WIDTH, HALF_WIDTH, D_MODEL, D_MODEL))
    w_cq, w_ckv, w_kr, w_qna, w_kna, w_vna, w_ga, w_gb = jnp.split(wi, splits[:-1].tolist(), axis=1)
    w_kr_blk = jnp.pad(w_kr, ((0, 0), (MLA_NOPE, LANES - MLA_QK)))
    kv_up = mla_w_ukv[0].reshape(KV_LORA, HEADS, MLA_NOPE + MLA_NOPE)
    p = {
        "g1": row(ffn1_norm[0]), "gmix": row(mix_norm[0]), "g2": row(ffn2_norm[0]),
        "win_r": jnp.concatenate([w_cq, w_ckv, w_kr_blk, w_qna, w_kna, w_vna], axis=1).astype(_BF),
        "qag": row(mla_q_a_norm[0]), "kvag": row(mla_kv_a_norm[0]),
        "wuq": _pad_heads(mla_w_uq[0], MLA_QK).astype(_BF),
        "wukv": jnp.concatenate([_pad_heads(kv_up[:, :, :MLA_NOPE].reshape(KV_LORA, -1), MLA_NOPE),
                                 kv_up[:, :, MLA_NOPE:].reshape(KV_LORA, -1)], axis=1).astype(_BF),
        "gq": _lane_row(mla_q_head_norm[0] * MLA_QK ** -0.5, 0),
        "gkn": _lane_row(mla_k_head_norm[0, :MLA_NOPE], 0),
        "gkr": _lane_row(mla_k_head_norm[0, MLA_NOPE:], MLA_NOPE),
        "gqna": _lane_row(na_q_head_norm[0] * NA_HEAD_DIM ** -0.5, 0),
        "gkna": _lane_row(na_k_head_norm[0], 0),
        "wga": w_ga.astype(_BF), "wgb": w_gb.astype(_BF),
        "wba": w_branch_a[0].astype(_BF), "wbb": w_branch_b[0].astype(_BF), "wout": w_out[0].astype(_BF),
    }
    p["wg1"], p["wu1"], p["wd1"] = _ff_chunks(ffn1_w_gate[0], ffn1_w_up[0], ffn1_w_down[0])
    p["wg2"], p["wu2"], p["wd2"] = _ff_chunks(ffn2_w_gate[0], ffn2_w_up[0], ffn2_w_down[0])
    assert p["win_r"].shape[1] == _IN_R_WIDTH

    tabs = _rope_tables()
    meta = _ffn1_proj(meta_tokens.astype(_F32), [t[:N_META] for t in tabs], p, N_META, 40)
    _, _, km_meta, vm_meta, _, kn_meta, vn_meta = meta
    real = _ffn1_proj(x.reshape(b * SEQ, D_MODEL), [t[N_META:] for t in tabs], p, TOKEN_TILE, 56)
    h1, qm, km, vm, qn, kn, vn = real
    seq = lambda a: a.reshape(b, SEQ, HEADS * LANES)
    o_a = _mla_attn(seq(qm), seq(km), seq(vm), km_meta, vm_meta)
    o_b = _na_attn(seq(qn), seq(kn), seq(vn), kn_meta, vn_meta, _na_bias_table(na_rel_bias[0]))
    out = _merge_ffn2(h1, o_a.reshape(b * SEQ, HALF_WIDTH), o_b.reshape(b * SEQ, HALF_WIDTH), p)
    return out.reshape(b, SEQ, D_MODEL)
```

```python
import math

import jax
import jax.numpy as jnp
import numpy as np
from jax import lax
from jax.experimental import pallas as pl
from jax.experimental.pallas import tpu as pltpu

D_MODEL = 1024
SEQ = 4096
N_META = 16
GRID_W = 64
HEADS = 8
MLA_NOPE = 64
MLA_ROPE = 32
MLA_QK = MLA_NOPE + MLA_ROPE
Q_LORA = 384
KV_LORA = 256
NA_HEAD_DIM = 64
NA_MAX_ROWS = 8
NA_KW = 16
D_FF = 2816
ROPE_THETA = 10000.0
EPS = 1e-6
HALF_WIDTH = 512
LOG2E = math.log2(math.e)

LANES = 128
HALF_LANES = LANES // 2
FF_CHUNK = 256
N_FF_CHUNKS = D_FF // FF_CHUNK
assert N_FF_CHUNKS * FF_CHUNK == D_FF

TOKEN_TILE = 512
MLA_Q_TILE = 256
MLA_K_TILE = 256
MLA_PV_LAG = 24
NA_GROUP_ROWS = 4
NA_GROUP = NA_GROUP_ROWS * GRID_W
NA_SLAB_ROWS = NA_GROUP_ROWS + NA_MAX_ROWS
NA_SLAB = NA_SLAB_ROWS * GRID_W
NA_ROWS = SEQ // GRID_W
NA_GROUPS = NA_ROWS // NA_GROUP_ROWS
MASKED = -1e30

_O_CQ, _O_CKV, _O_KR = 0, Q_LORA, Q_LORA + KV_LORA
_O_QNA = _O_KR + LANES
_O_KNA = _O_QNA + HALF_WIDTH
_O_VNA = _O_KNA + HALF_WIDTH
_IN_R_WIDTH = _O_VNA + HALF_WIDTH

_BF = jnp.bfloat16
_F32 = jnp.float32


def _dot(a, b):
    return jnp.dot(a, b, preferred_element_type=_F32)


def _dot_nt(a, b):
    return lax.dot_general(a, b, (((1,), (1,)), ((), ())), preferred_element_type=_F32)


def _sigmoid(x):
    return 1.0 / (1.0 + jnp.exp(-x))


def _rms(x, g):
    return x * lax.rsqrt(jnp.mean(x * x, axis=-1, keepdims=True) + EPS) * g


def _lane_iota(rows):
    return lax.broadcasted_iota(jnp.int32, (rows, LANES), 1)


def _swiglu(xn, wg_ref, wu_ref, wd_ref, act_ref):
    for c in range(N_FF_CHUNKS):
        g = _dot(xn, wg_ref[c])
        u = _dot(xn, wu_ref[c])
        act_ref[c] = (g * _sigmoid(g) * u).astype(_BF)
    y = _dot(act_ref[0], wd_ref[0])
    for c in range(1, N_FF_CHUNKS):
        y += _dot(act_ref[c], wd_ref[c])
    return y


def _rope(x, c, s):
    return x * c + pltpu.roll(x, LANES - MLA_ROPE, 1) * s


def _half_head_norm(t, g):
    lo = _lane_iota(t.shape[0]) < HALF_LANES
    out = []
    for j in range(t.shape[1] // LANES):
        blk = t[:, LANES * j:LANES * (j + 1)]
        sq = blk * blk
        s_lo = jnp.sum(jnp.where(lo, sq, 0.0), axis=-1, keepdims=True)
        s_hi = jnp.sum(jnp.where(lo, 0.0, sq), axis=-1, keepdims=True)
        r = lax.rsqrt(jnp.where(lo, s_lo, s_hi) * (1.0 / NA_HEAD_DIM) + EPS)
        out.append((blk * r * g).astype(_BF))
    return jnp.concatenate(out, axis=1)


def _ffn1_proj_kernel(x_ref, g1_ref, wg_ref, wu_ref, wd_ref, gmix_ref, win_ref, qag_ref, wuq_ref,
                      kvag_ref, wukv_ref, gq_ref, gkn_ref, gkr_ref, gqna_ref, gkna_ref, rc_ref, rs_ref,
                      h1_ref, qm_ref, km_ref, vm_ref, qn_ref, kn_ref, vn_ref, act_ref):
    x = x_ref[...]
    h1 = x + 0.5 * _swiglu(_rms(x, g1_ref[...]).astype(_BF), wg_ref, wu_ref, wd_ref, act_ref)
    h1_ref[...] = h1
    xn = _rms(h1, gmix_ref[...]).astype(_BF)

    qk_lanes = _lane_iota(x.shape[0]) < MLA_QK
    rc, rs = rc_ref[...], rs_ref[...]

    cq = _dot(xn, win_ref[:, _O_CQ:_O_CQ + Q_LORA])
    q = _dot(_rms(cq, qag_ref[...]).astype(_BF), wuq_ref[...])
    gq = gq_ref[...]
    for h in range(HEADS):
        qh = q[:, LANES * h:LANES * (h + 1)]
        ss = jnp.sum(jnp.where(qk_lanes, qh * qh, 0.0), axis=-1, keepdims=True)
        r = lax.rsqrt(ss * (1.0 / MLA_QK) + EPS)
        qm_ref[:, LANES * h:LANES * (h + 1)] = _rope(qh * r * gq, rc, rs).astype(_BF)

    ckv = _dot(xn, win_ref[:, _O_CKV:_O_CKV + KV_LORA])
    kvu = _dot(_rms(ckv, kvag_ref[...]).astype(_BF), wukv_ref[...])
    kr = _dot(xn, win_ref[:, _O_KR:_O_KR + LANES])
    ss_r = jnp.sum(jnp.where(qk_lanes, kr * kr, 0.0), axis=-1, keepdims=True)
    krr = _rope(kr * gkr_ref[...], rc, rs)
    gkn = gkn_ref[...]
    for h in range(HEADS):
        kh = kvu[:, LANES * h:LANES * (h + 1)]
        r = lax.rsqrt((jnp.sum(kh * kh, axis=-1, keepdims=True) + ss_r) * (1.0 / MLA_QK) + EPS)
        km_ref[:, LANES * h:LANES * (h + 1)] = ((kh * gkn + krr) * r).astype(_BF)
    vm_ref[...] = kvu[:, HEADS * LANES:].astype(_BF)

    qn_ref[...] = _half_head_norm(_dot(xn, win_ref[:, _O_QNA:_O_QNA + HALF_WIDTH]), gqna_ref[...])
    kn_ref[...] = _half_head_norm(_dot(xn, win_ref[:, _O_KNA:_O_KNA + HALF_WIDTH]), gkna_ref[...])
    vn_ref[...] = _dot(xn, win_ref[:, _O_VNA:_O_VNA + HALF_WIDTH]).astype(_BF)


def _const_spec(shape):
    nd = len(shape)
    return pl.BlockSpec(shape, lambda i: (0,) * nd, pipeline_mode=pl.Buffered(1))


def _ffn1_proj(x2d, rope_tabs, p, tile, vmem_mb):
    n = x2d.shape[0]
    tabs_per_seq = rope_tabs[0].shape[0] // tile
    row_spec = lambda w: pl.BlockSpec((tile, w), lambda i: (i, 0))
    tab_spec = pl.BlockSpec((tile, LANES), lambda i: (i % tabs_per_seq, 0))
    consts = [p["g1"], p["wg1"], p["wu1"], p["wd1"], p["gmix"], p["win_r"], p["qag"], p["wuq"],
              p["kvag"], p["wukv"], p["gq"], p["gkn"], p["gkr"], p["gqna"], p["gkna"]]
    widths = [HEADS * LANES, HEADS * LANES, HALF_WIDTH, HALF_WIDTH, HALF_WIDTH, HALF_WIDTH]
    out_shape = [jax.ShapeDtypeStruct((n, D_MODEL), _F32)] + [jax.ShapeDtypeStruct((n, w), _BF) for w in widths]
    return pl.pallas_call(
        _ffn1_proj_kernel,
        grid=(n // tile,),
        in_specs=[row_spec(D_MODEL)] + [_const_spec(c.shape) for c in consts] + [tab_spec] * 2,
        out_specs=[row_spec(D_MODEL)] + [row_spec(w) for w in widths],
        out_shape=out_shape,
        scratch_shapes=[pltpu.VMEM((N_FF_CHUNKS, tile, FF_CHUNK), _BF)],
        compiler_params=pltpu.CompilerParams(dimension_semantics=("arbitrary",),
                                             vmem_limit_bytes=vmem_mb * 2 ** 20),
        name="ffn1_proj",
    )(x2d, *consts, *rope_tabs)


def _extend_pair(v_pair, dst_ref):
    dst_ref[:, :LANES] = v_pair
    dst_ref[:, LANES:] = jnp.ones_like(v_pair)


def _mla_kernel(q_ref, k_ref, v_ref, km_ref, vm_ref, o_ref, vext_ref, vmext_ref):
    @pl.when(pl.program_id(1) == 0)
    def _():
        for hp in range(HEADS // 2):
            _extend_pair(v_ref[0, :, LANES * hp:LANES * (hp + 1)], vext_ref.at[hp])
            _extend_pair(vm_ref[:, LANES * hp:LANES * (hp + 1)], vmext_ref.at[hp])

    n_tiles = SEQ // MLA_K_TILE
    slots = HEADS * n_tiles
    scores, meta_scores, row_max, acc, outs = {}, {}, {}, {}, {}
    run_max = None
    for g in range(slots + MLA_PV_LAG):
        if g < slots:
            h, t = divmod(g, n_tiles)
            sl = slice(LANES * h, LANES * (h + 1))
            if t == 0:
                q = q_ref[0, :, sl]
                meta_scores[h] = _dot_nt(q, km_ref[:, sl])
                scores[h] = []
                run_max = None
            s = _dot_nt(q, k_ref[0, MLA_K_TILE * t:MLA_K_TILE * (t + 1), sl])
            run_max = s if run_max is None else jnp.maximum(run_max, s)
            scores[h].append(s)
            if t == n_tiles - 1:
                row_max[h] = jnp.maximum(jnp.max(run_max, axis=-1, keepdims=True),
                                         jnp.max(meta_scores[h], axis=-1, keepdims=True))
        gp = g - MLA_PV_LAG
        if gp >= 0:
            h, t = divmod(gp, n_tiles)
            if t == 0:
                acc[h] = _dot(jnp.exp2(meta_scores[h] - row_max[h]).astype(_BF), vmext_ref[h // 2])
            acc[h] += _dot(jnp.exp2(scores[h][t] - row_max[h]).astype(_BF),
                           vext_ref[h // 2, MLA_K_TILE * t:MLA_K_TILE * (t + 1), :])
            if t == n_tiles - 1:
                outs[h] = acc[h][:, :LANES] / acc[h][:, LANES:LANES + 1]
    lo = _lane_iota(MLA_Q_TILE) < HALF_LANES
    for hp in range(HEADS // 2):
        o_ref[0, :, LANES * hp:LANES * (hp + 1)] = jnp.where(lo, outs[2 * hp], outs[2 * hp + 1]).astype(_BF)


def _mla_attn(q, k, v, k_meta, v_meta):
    b = q.shape[0]
    once = pl.Buffered(1)
    whole = lambda shape: pl.BlockSpec(shape, lambda bi, i: (0,) * len(shape), pipeline_mode=once)
    return pl.pallas_call(
        _mla_kernel,
        grid=(b, SEQ // MLA_Q_TILE),
        in_specs=[pl.BlockSpec((1, MLA_Q_TILE, HEADS * LANES), lambda bi, i: (bi, i, 0)),
                  pl.BlockSpec((1, SEQ, HEADS * LANES), lambda bi, i: (bi, 0, 0), pipeline_mode=once),
                  pl.BlockSpec((1, SEQ, HALF_WIDTH), lambda bi, i: (bi, 0, 0), pipeline_mode=once),
                  whole((N_META, HEADS * LANES)), whole((N_META, HALF_WIDTH))],
        out_specs=pl.BlockSpec((1, MLA_Q_TILE, HALF_WIDTH), lambda bi, i: (bi, i, 0)),
        out_shape=jax.ShapeDtypeStruct((b, SEQ, HALF_WIDTH), _BF),
        scratch_shapes=[pltpu.VMEM((HEADS // 2, SEQ, 2 * LANES), _BF),
                        pltpu.VMEM((HEADS // 2, N_META, 2 * LANES), _BF)],
        compiler_params=pltpu.CompilerParams(dimension_semantics=("arbitrary",) * 2,
                                             vmem_limit_bytes=56 * 2 ** 20),
        name="mla_attn",
    )(q, k, v, k_meta, v_meta)


def _na_kernel(q_ref, k_ref, v_ref, km_ref, vm_ref, tab_ref, o_ref, vext_ref, vmext_ref):
    _extend_pair(v_ref[0], vext_ref)
    _extend_pair(vm_ref[...], vmext_ref)
    lo = _lane_iota(NA_GROUP) < HALF_LANES
    k_meta = km_ref[...]
    v_meta = vmext_ref[...]

    def group(g, carry):
        slab = pl.multiple_of(jnp.clip(g * NA_GROUP_ROWS - NA_MAX_ROWS // 2, 0, NA_ROWS - NA_SLAB_ROWS) * GRID_W,
                              GRID_W)
        q0 = pl.multiple_of(g * NA_GROUP, NA_GROUP)
        variant = jnp.where(g == 0, 0, jnp.where(g == NA_GROUPS - 1, 2, 1))
        q_pair = q_ref[0, pl.ds(q0, NA_GROUP), :]
        k_slab = k_ref[0, pl.ds(slab, NA_SLAB), :]
        v_slab = vext_ref[pl.ds(slab, NA_SLAB), :]
        outs = []
        for hh in range(2):
            q = jnp.where(lo == (hh == 0), q_pair, jnp.zeros_like(q_pair))
            s = _dot_nt(q, k_slab) + tab_ref[hh, variant]
            sm = _dot_nt(q, k_meta)
            m = jnp.maximum(jnp.max(s, axis=-1, keepdims=True), jnp.max(sm, axis=-1, keepdims=True))
            acc = _dot(jnp.exp2(s - m).astype(_BF), v_slab) + _dot(jnp.exp2(sm - m).astype(_BF), v_meta)
            outs.append(acc[:, :LANES] / acc[:, LANES:LANES + 1])
        o_ref[0, pl.ds(q0, NA_GROUP), :] = jnp.where(lo, outs[0], outs[1]).astype(_BF)
        return carry

    lax.fori_loop(0, NA_GROUPS, group, 0, unroll=2)


def _na_attn(q, k, v, k_meta, v_meta, table):
    b = q.shape[0]
    seq_spec = pl.BlockSpec((1, SEQ, LANES), lambda hp, bi: (bi, 0, hp))
    meta_spec = pl.BlockSpec((N_META, LANES), lambda hp, bi: (0, hp))
    return pl.pallas_call(
        _na_kernel,
        grid=(HEADS // 2, b),
        in_specs=[seq_spec, seq_spec, seq_spec, meta_spec, meta_spec,
                  pl.BlockSpec((2, 3, NA_GROUP, NA_SLAB), lambda hp, bi: (hp, 0, 0, 0))],
        out_specs=seq_spec,
        out_shape=jax.ShapeDtypeStruct((b, SEQ, HALF_WIDTH), _BF),
        scratch_shapes=[pltpu.VMEM((SEQ, 2 * LANES), _BF), pltpu.VMEM((N_META, 2 * LANES), _BF)],
        compiler_params=pltpu.CompilerParams(dimension_semantics=("arbitrary",) * 2,
                                             vmem_limit_bytes=48 * 2 ** 20),
        name="na_attn",
    )(q, k, v, k_meta, v_meta, table)


def _na_bias_table(rel_bias):
    c = np.arange(GRID_W)
    cs = np.clip(c - NA_KW // 2, 0, GRID_W - NA_KW)
    col_ok = (c[None, :] >= cs[:, None]) & (c[None, :] < cs[:, None] + NA_KW)
    reach = GRID_W - NA_KW
    padded = jnp.pad(rel_bias * LOG2E, ((0, 0), (0, 0), (reach, reach)))
    start = NA_KW - 1 + reach
    by_col = jnp.stack([padded[:, :, start - ci:start - ci + GRID_W] for ci in range(GRID_W)], axis=2)
    by_col = jnp.where(col_ok[None, None], by_col, MASKED)
    masked_blk = jnp.full_like(by_col[:, 0], MASKED)
    tabs = []
    for g in (0, 1, NA_GROUPS - 1):
        slab_row0 = int(np.clip(g * NA_GROUP_ROWS - NA_MAX_ROWS // 2, 0, NA_ROWS - NA_SLAB_ROWS))
        rows = []
        for i in range(NA_GROUP_ROWS):
            r = g * NA_GROUP_ROWS + i
            rs = int(np.clip(r - NA_MAX_ROWS // 2, 0, NA_ROWS - NA_MAX_ROWS))
            blks = []
            for j in range(NA_SLAB_ROWS):
                kr = slab_row0 + j
                blks.append(by_col[:, kr - r + NA_MAX_ROWS - 1] if rs <= kr < rs + NA_MAX_ROWS else masked_blk)
            rows.append(jnp.concatenate(blks, axis=2))
        tabs.append(jnp.concatenate(rows, axis=1))
    return jnp.stack(tabs, axis=1)


def _merge_ffn2_kernel(h1_ref, oa_ref, ob_ref, gmix_ref, wga_ref, wgb_ref, wba_ref, wbb_ref, wout_ref,
                       g2_ref, wg_ref, wu_ref, wd_ref, out_ref, act_ref):
    h1 = h1_ref[...]
    xn = _rms(h1, gmix_ref[...]).astype(_BF)
    merged = (_sigmoid(_dot(xn, wga_ref[...])) * _dot(oa_ref[...], wba_ref[...])
              + _sigmoid(_dot(xn, wgb_ref[...])) * _dot(ob_ref[...], wbb_ref[...]))
    h2 = h1 + _dot(merged.astype(_BF), wout_ref[...])
    out_ref[...] = h2 + 0.5 * _swiglu(_rms(h2, g2_ref[...]).astype(_BF), wg_ref, wu_ref, wd_ref, act_ref)


def _merge_ffn2(h1, oa, ob, p):
    n = h1.shape[0]
    row_spec = lambda w: pl.BlockSpec((TOKEN_TILE, w), lambda i: (i, 0))
    consts = [p["gmix"], p["wga"], p["wgb"], p["wba"], p["wbb"], p["wout"], p["g2"], p["wg2"], p["wu2"], p["wd2"]]
    return pl.pallas_call(
        _merge_ffn2_kernel,
        grid=(n // TOKEN_TILE,),
        in_specs=[row_spec(D_MODEL), row_spec(HALF_WIDTH), row_spec(HALF_WIDTH)]
                 + [_const_spec(c.shape) for c in consts],
        out_specs=row_spec(D_MODEL),
        out_shape=jax.ShapeDtypeStruct((n, D_MODEL), _F32),
        scratch_shapes=[pltpu.VMEM((N_FF_CHUNKS, TOKEN_TILE, FF_CHUNK), _BF)],
        compiler_params=pltpu.CompilerParams(dimension_semantics=("arbitrary",),
                                             vmem_limit_bytes=56 * 2 ** 20),
        name="merge_ffn2",
    )(h1, oa, ob, *consts)


def _rope_swapped(w):
    half = MLA_ROPE // 2
    return jnp.concatenate([w, w[..., half:], w[..., :half]], axis=-1)


def _lane_row(v, offset):
    return jnp.zeros((1, LANES), _F32).at[0, offset:offset + v.shape[0]].set(v)


def _ff_chunks(w_gate, w_up, w_down):
    cols = lambda w: w.reshape(D_MODEL, N_FF_CHUNKS, FF_CHUNK).transpose(1, 0, 2).astype(_BF)
    return cols(w_gate), cols(w_up), w_down.reshape(N_FF_CHUNKS, FF_CHUNK, D_MODEL).astype(_BF)


def _rope_tables():
    half = MLA_ROPE // 2
    pos = jnp.arange(N_META + SEQ, dtype=_F32)
    inv = ROPE_THETA ** (-jnp.arange(half, dtype=_F32) / half)
    ang = pos[:, None] * inv[None, :]
    cos, sin = jnp.cos(ang), jnp.sin(ang)
    n = pos.shape[0]
    tail = jnp.zeros((n, LANES - MLA_QK), _F32)
    c = jnp.concatenate([jnp.ones((n, MLA_NOPE), _F32), cos, cos, tail], axis=1)
    s = jnp.concatenate([jnp.zeros((n, MLA_NOPE), _F32), -sin, sin, tail], axis=1)
    return c, s


def kernel(x, meta_tokens, ffn1_norm, ffn1_w_gate, ffn1_w_up, ffn1_w_down, mix_norm, w_in, mla_q_a_norm, mla_w_uq,
           mla_kv_a_norm, mla_w_ukv, mla_q_head_norm, mla_k_head_norm, na_q_head_norm, na_k_head_norm, na_rel_bias,
           w_branch_a, w_branch_b, w_out, ffn2_norm, ffn2_w_gate, ffn2_w_up, ffn2_w_down):
    assert ffn1_norm.shape[0] == 1, "single-layer block"
    b = x.shape[0]
    row = lambda v: v.reshape(1, -1).astype(_F32)
    splits = np.cumsum((Q_LORA, KV_LORA, MLA_ROPE, HALF_WIDTH, HALF_WIDTH, HALF_WIDTH, D_MODEL, D_MODEL))
    w_cq, w_ckv, w_kr, w_qna, w_kna, w_vna, w_ga, w_gb = jnp.split(w_in[0], splits[:-1].tolist(), axis=1)
    w_kr_blk = jnp.pad(_rope_swapped(w_kr), ((0, 0), (MLA_NOPE, 0)))
    uq = mla_w_uq[0].reshape(Q_LORA, HEADS, MLA_QK)
    uq = jnp.concatenate([uq[..., :MLA_NOPE], _rope_swapped(uq[..., MLA_NOPE:])], axis=-1).reshape(Q_LORA, -1)
    kv_up = mla_w_ukv[0].reshape(KV_LORA, HEADS, 2 * MLA_NOPE)
    k_nope_up = jnp.pad(kv_up[..., :MLA_NOPE], ((0, 0), (0, 0), (0, LANES - MLA_NOPE))).reshape(KV_LORA, -1)
    gq_mla = mla_q_head_norm[0] * (MLA_QK ** -0.5 * LOG2E)
    gk_mla = mla_k_head_norm[0]
    gq_na = na_q_head_norm[0] * (NA_HEAD_DIM ** -0.5 * LOG2E)
    p = {
        "g1": row(ffn1_norm[0]), "gmix": row(mix_norm[0]), "g2": row(ffn2_norm[0]),
        "win_r": jnp.concatenate([w_cq, w_ckv, w_kr_blk, w_qna, w_kna, w_vna], axis=1).astype(_BF),
        "qag": row(mla_q_a_norm[0]), "kvag": row(mla_kv_a_norm[0]),
        "wuq": uq.astype(_BF),
        "wukv": jnp.concatenate([k_nope_up, kv_up[..., MLA_NOPE:].reshape(KV_LORA, -1)], axis=1).astype(_BF),
        "gq": row(jnp.concatenate([gq_mla[:MLA_NOPE], _rope_swapped(gq_mla[MLA_NOPE:])])),
        "gkn": _lane_row(gk_mla[:MLA_NOPE], 0),
        "gkr": _lane_row(_rope_swapped(gk_mla[MLA_NOPE:]), MLA_NOPE),
        "gqna": row(jnp.concatenate([gq_na, gq_na])),
        "gkna": row(jnp.concatenate([na_k_head_norm[0], na_k_head_norm[0]])),
        "wga": w_ga.astype(_BF), "wgb": w_gb.astype(_BF),
        "wba": w_branch_a[0].astype(_BF), "wbb": w_branch_b[0].astype(_BF), "wout": w_out[0].astype(_BF),
    }
    p["wg1"], p["wu1"], p["wd1"] = _ff_chunks(ffn1_w_gate[0], ffn1_w_up[0], ffn1_w_down[0])
    p["wg2"], p["wu2"], p["wd2"] = _ff_chunks(ffn2_w_gate[0], ffn2_w_up[0], ffn2_w_down[0])
    assert p["win_r"].shape[1] == _IN_R_WIDTH and p["wuq"].shape[1] == HEADS * LANES

    tabs = _rope_tables()
    meta = _ffn1_proj(meta_tokens.astype(_F32), [t[:N_META] for t in tabs], p, N_META, 40)
    _, _, km_meta, vm_meta, _, kn_meta, vn_meta = meta
    real = _ffn1_proj(x.reshape(b * SEQ, D_MODEL), [t[N_META:] for t in tabs], p, TOKEN_TILE, 56)
    h1, qm, km, vm, qn, kn, vn = real
    seq = lambda a: a.reshape(b, SEQ, a.shape[-1])
    o_a = _mla_attn(seq(qm), seq(km), seq(vm), km_meta, vm_meta)
    o_b = _na_attn(seq(qn), seq(kn), seq(vn), kn_meta, vn_meta, _na_bias_table(na_rel_bias[0]))
    out = _merge_ffn2(h1, o_a.reshape(b * SEQ, HALF_WIDTH), o_b.reshape(b * SEQ, HALF_WIDTH), p)
    return out.reshape(b, SEQ, D_MODEL)
```

```python
import math

import jax
import jax.numpy as jnp
import numpy as np
from jax import lax
from jax.experimental import pallas as pl
from jax.experimental.pallas import tpu as pltpu

D_MODEL = 1024
SEQ = 4096
N_META = 16
GRID_W = 64
HEADS = 8
MLA_NOPE = 64
MLA_ROPE = 32
MLA_QK = MLA_NOPE + MLA_ROPE
Q_LORA = 384
KV_LORA = 256
NA_HEAD_DIM = 64
NA_MAX_ROWS = 8
NA_KW = 16
D_FF = 2816
ROPE_THETA = 10000.0
EPS = 1e-6
HALF_WIDTH = 512
LOG2E = math.log2(math.e)

LANES = 128
HALF_LANES = LANES // 2
FF_CHUNK = 256
N_FF_CHUNKS = D_FF // FF_CHUNK
assert N_FF_CHUNKS * FF_CHUNK == D_FF

TOKEN_TILE = 512
MLA_Q_TILE = 256
MLA_K_TILE = 256
MLA_GROUP_TILES = 8
MLA_PV_LAG = 12
NA_GROUP_ROWS = 4
NA_GROUP = NA_GROUP_ROWS * GRID_W
NA_WIN_ROWS = NA_GROUP_ROWS + NA_MAX_ROWS - 1
NA_WIN = NA_WIN_ROWS * GRID_W
NA_KEYS = 768
assert NA_WIN + N_META <= NA_KEYS and NA_KEYS % 256 == 0
NA_ROWS = SEQ // GRID_W
NA_GROUPS = NA_ROWS // NA_GROUP_ROWS
MASKED = -1e30

_O_CQ, _O_CKV, _O_KR = 0, Q_LORA, Q_LORA + KV_LORA
_O_QNA = _O_KR + LANES
_O_KNA = _O_QNA + HALF_WIDTH
_O_VNA = _O_KNA + HALF_WIDTH
_IN_R_WIDTH = _O_VNA + HALF_WIDTH

_BF = jnp.bfloat16
_F32 = jnp.float32


def _dot(a, b):
    return jnp.dot(a, b, preferred_element_type=_F32)


def _dot_nt(a, b):
    return lax.dot_general(a, b, (((1,), (1,)), ((), ())), preferred_element_type=_F32)


def _sigmoid(x):
    return 1.0 / (1.0 + jnp.exp(-x))


def _rms(x, g):
    return x * lax.rsqrt(jnp.mean(x * x, axis=-1, keepdims=True) + EPS) * g


def _lane_iota(rows):
    return lax.broadcasted_iota(jnp.int32, (rows, LANES), 1)


def _swiglu(xn, wg_ref, wu_ref, wd_ref, act_ref):
    for c in range(N_FF_CHUNKS):
        g = _dot(xn, wg_ref[c])
        u = _dot(xn, wu_ref[c])
        act_ref[c] = (g * _sigmoid(g) * u).astype(_BF)
    y = _dot(act_ref[0], wd_ref[0])
    for c in range(1, N_FF_CHUNKS):
        y += _dot(act_ref[c], wd_ref[c])
    return y


def _rope(x, c, s):
    return x * c + pltpu.roll(x, LANES - MLA_ROPE, 1) * s


def _half_head_norm(t, g):
    lo = _lane_iota(t.shape[0]) < HALF_LANES
    out = []
    for j in range(t.shape[1] // LANES):
        blk = t[:, LANES * j:LANES * (j + 1)]
        sq = blk * blk
        s_lo = jnp.sum(jnp.where(lo, sq, 0.0), axis=-1, keepdims=True)
        s_hi = jnp.sum(jnp.where(lo, 0.0, sq), axis=-1, keepdims=True)
        r = lax.rsqrt(jnp.where(lo, s_lo, s_hi) * (1.0 / NA_HEAD_DIM) + EPS)
        out.append((blk * r * g).astype(_BF))
    return jnp.concatenate(out, axis=1)


def _ffn1_proj_kernel(x_ref, g1_ref, wg_ref, wu_ref, wd_ref, gmix_ref, win_ref, qag_ref, wuq_ref,
                      kvag_ref, wukv_ref, gq_ref, gkn_ref, gkr_ref, gqna_ref, gkna_ref, rc_ref, rs_ref,
                      h1_ref, qm_ref, km_ref, vm_ref, qn_ref, kn_ref, vn_ref, act_ref):
    x = x_ref[...]
    h1 = x + 0.5 * _swiglu(_rms(x, g1_ref[...]).astype(_BF), wg_ref, wu_ref, wd_ref, act_ref)
    h1_ref[...] = h1
    xn = _rms(h1, gmix_ref[...]).astype(_BF)

    qk_lanes = _lane_iota(x.shape[0]) < MLA_QK
    rc, rs = rc_ref[...], rs_ref[...]

    cq = _dot(xn, win_ref[:, _O_CQ:_O_CQ + Q_LORA])
    q = _dot(_rms(cq, qag_ref[...]).astype(_BF), wuq_ref[...])
    gq = gq_ref[...]
    for h in range(HEADS):
        qh = q[:, LANES * h:LANES * (h + 1)]
        ss = jnp.sum(jnp.where(qk_lanes, qh * qh, 0.0), axis=-1, keepdims=True)
        r = lax.rsqrt(ss * (1.0 / MLA_QK) + EPS)
        qm_ref[:, LANES * h:LANES * (h + 1)] = _rope(qh * r * gq, rc, rs).astype(_BF)

    ckv = _dot(xn, win_ref[:, _O_CKV:_O_CKV + KV_LORA])
    kvu = _dot(_rms(ckv, kvag_ref[...]).astype(_BF), wukv_ref[...])
    kr = _dot(xn, win_ref[:, _O_KR:_O_KR + LANES])
    ss_r = jnp.sum(jnp.where(qk_lanes, kr * kr, 0.0), axis=-1, keepdims=True)
    krr = _rope(kr * gkr_ref[...], rc, rs)
    gkn = gkn_ref[...]
    for h in range(HEADS):
        kh = kvu[:, LANES * h:LANES * (h + 1)]
        r = lax.rsqrt((jnp.sum(kh * kh, axis=-1, keepdims=True) + ss_r) * (1.0 / MLA_QK) + EPS)
        km_ref[:, LANES * h:LANES * (h + 1)] = ((kh * gkn + krr) * r).astype(_BF)
    vm_ref[...] = kvu[:, HEADS * LANES:].astype(_BF)

    qn_ref[...] = _half_head_norm(_dot(xn, win_ref[:, _O_QNA:_O_QNA + HALF_WIDTH]), gqna_ref[...])
    kn_ref[...] = _half_head_norm(_dot(xn, win_ref[:, _O_KNA:_O_KNA + HALF_WIDTH]), gkna_ref[...])
    vn_ref[...] = _dot(xn, win_ref[:, _O_VNA:_O_VNA + HALF_WIDTH]).astype(_BF)


def _const_spec(shape):
    nd = len(shape)
    return pl.BlockSpec(shape, lambda i: (0,) * nd, pipeline_mode=pl.Buffered(1))


def _ffn1_proj(x2d, rope_tabs, p, tile, vmem_mb):
    n = x2d.shape[0]
    tabs_per_seq = rope_tabs[0].shape[0] // tile
    row_spec = lambda w: pl.BlockSpec((tile, w), lambda i: (i, 0))
    tab_spec = pl.BlockSpec((tile, LANES), lambda i: (i % tabs_per_seq, 0))
    consts = [p["g1"], p["wg1"], p["wu1"], p["wd1"], p["gmix"], p["win_r"], p["qag"], p["wuq"],
              p["kvag"], p["wukv"], p["gq"], p["gkn"], p["gkr"], p["gqna"], p["gkna"]]
    widths = [HEADS * LANES, HEADS * LANES, HALF_WIDTH, HALF_WIDTH, HALF_WIDTH, HALF_WIDTH]
    out_shape = [jax.ShapeDtypeStruct((n, D_MODEL), _F32)] + [jax.ShapeDtypeStruct((n, w), _BF) for w in widths]
    return pl.pallas_call(
        _ffn1_proj_kernel,
        grid=(n // tile,),
        in_specs=[row_spec(D_MODEL)] + [_const_spec(c.shape) for c in consts] + [tab_spec] * 2,
        out_specs=[row_spec(D_MODEL)] + [row_spec(w) for w in widths],
        out_shape=out_shape,
        scratch_shapes=[pltpu.VMEM((N_FF_CHUNKS, tile, FF_CHUNK), _BF)],
        compiler_params=pltpu.CompilerParams(dimension_semantics=("arbitrary",),
                                             vmem_limit_bytes=vmem_mb * 2 ** 20),
        name="ffn1_proj",
    )(x2d, *consts, *rope_tabs)


def _extend_pair(v_pair, dst_ref):
    dst_ref[:, :LANES] = v_pair
    dst_ref[:, LANES:] = jnp.ones_like(v_pair)


def _mla_kernel(q_ref, k_ref, v_ref, km_ref, vm_ref, o_ref, vext_ref, vmext_ref):
    @pl.when(pl.program_id(1) == 0)
    def _():
        for hp in range(HEADS // 2):
            _extend_pair(v_ref[0, :, LANES * hp:LANES * (hp + 1)], vext_ref.at[hp])
            _extend_pair(vm_ref[:, LANES * hp:LANES * (hp + 1)], vmext_ref.at[hp])

    n_tiles = SEQ // MLA_K_TILE
    n_groups = n_tiles // MLA_GROUP_TILES
    slots = HEADS * n_tiles
    scores, meta_scores, group_max, parts, outs = {}, {}, {}, {}, {}
    run_max = None
    for g in range(slots + MLA_PV_LAG):
        if g < slots:
            h, t = divmod(g, n_tiles)
            sl = slice(LANES * h, LANES * (h + 1))
            if t == 0:
                q = q_ref[0, :, sl]
                meta_scores[h] = _dot_nt(q, km_ref[:, sl])
            s = _dot_nt(q, k_ref[0, MLA_K_TILE * t:MLA_K_TILE * (t + 1), sl])
            run_max = s if t % MLA_GROUP_TILES == 0 else jnp.maximum(run_max, s)
            scores[(h, t)] = s
            if t % MLA_GROUP_TILES == MLA_GROUP_TILES - 1:
                m = jnp.max(run_max, axis=-1, keepdims=True)
                if t // MLA_GROUP_TILES == 0:
                    m = jnp.maximum(m, jnp.max(meta_scores[h], axis=-1, keepdims=True))
                group_max[(h, t // MLA_GROUP_TILES)] = m
        gp = g - MLA_PV_LAG
        if gp >= 0:
            h, t = divmod(gp, n_tiles)
            j = t // MLA_GROUP_TILES
            m = group_max[(h, j)]
            part = _dot(jnp.exp2(scores.pop((h, t)) - m).astype(_BF),
                        vext_ref[h // 2, MLA_K_TILE * t:MLA_K_TILE * (t + 1), :])
            if t == 0:
                part += _dot(jnp.exp2(meta_scores[h] - m).astype(_BF), vmext_ref[h // 2])
            parts[(h, j)] = part if t % MLA_GROUP_TILES == 0 else parts[(h, j)] + part
            if t == n_tiles - 1:
                m_all = group_max[(h, 0)]
                for jj in range(1, n_groups):
                    m_all = jnp.maximum(m_all, group_max[(h, jj)])
                acc = sum(jnp.exp2(group_max[(h, jj)] - m_all) * parts.pop((h, jj)) for jj in range(n_groups))
                outs[h] = acc[:, :LANES] / acc[:, LANES:LANES + 1]
    lo = _lane_iota(MLA_Q_TILE) < HALF_LANES
    for hp in range(HEADS // 2):
        o_ref[0, :, LANES * hp:LANES * (hp + 1)] = jnp.where(lo, outs[2 * hp], outs[2 * hp + 1]).astype(_BF)


def _mla_attn(q, k, v, k_meta, v_meta):
    b = q.shape[0]
    once = pl.Buffered(1)
    whole = lambda shape: pl.BlockSpec(shape, lambda bi, i: (0,) * len(shape), pipeline_mode=once)
    return pl.pallas_call(
        _mla_kernel,
        grid=(b, SEQ // MLA_Q_TILE),
        in_specs=[pl.BlockSpec((1, MLA_Q_TILE, HEADS * LANES), lambda bi, i: (bi, i, 0)),
                  pl.BlockSpec((1, SEQ, HEADS * LANES), lambda bi, i: (bi, 0, 0), pipeline_mode=once),
                  pl.BlockSpec((1, SEQ, HALF_WIDTH), lambda bi, i: (bi, 0, 0), pipeline_mode=once),
                  whole((N_META, HEADS * LANES)), whole((N_META, HALF_WIDTH))],
        out_specs=pl.BlockSpec((1, MLA_Q_TILE, HALF_WIDTH), lambda bi, i: (bi, i, 0)),
        out_shape=jax.ShapeDtypeStruct((b, SEQ, HALF_WIDTH), _BF),
        scratch_shapes=[pltpu.VMEM((HEADS // 2, SEQ, 2 * LANES), _BF),
                        pltpu.VMEM((HEADS // 2, N_META, 2 * LANES), _BF)],
        compiler_params=pltpu.CompilerParams(dimension_semantics=("arbitrary",) * 2,
                                             vmem_limit_bytes=56 * 2 ** 20),
        name="mla_attn",
    )(q, k, v, k_meta, v_meta)


def _na_slab_row(g):
    return int(np.clip(g * NA_GROUP_ROWS - NA_MAX_ROWS // 2, 0, NA_ROWS - NA_WIN_ROWS))


def _na_kernel(q_ref, k_ref, v_ref, km_ref, vm_ref, tab_ref, o_ref, ks_ref, vs_ref):
    @pl.when(pl.program_id(1) == 0)
    def _():
        for buf in range(NA_GROUPS):
            ks_ref[buf, NA_WIN:, :] = jnp.zeros((NA_KEYS - NA_WIN, LANES), _BF)
            vs_ref[buf, NA_WIN:, :] = jnp.zeros((NA_KEYS - NA_WIN, 2 * LANES), _BF)
            ks_ref[buf, NA_WIN:NA_WIN + N_META, :] = km_ref[...]
            vs_ref[buf, NA_WIN:NA_WIN + N_META, :LANES] = vm_ref[...]
            vs_ref[buf, :NA_WIN + N_META, LANES:] = jnp.ones((NA_WIN + N_META, LANES), _BF)

    lo = _lane_iota(NA_GROUP) < HALF_LANES

    scores, outs = {}, {}
    items = 2 * NA_GROUPS
    for n in range(items + 1):
        if n < items:
            g, hh = divmod(n, 2)
            if hh == 0:
                rows = slice(_na_slab_row(g) * GRID_W, _na_slab_row(g) * GRID_W + NA_WIN)
                ks_ref[g, :NA_WIN, :] = k_ref[0, rows, :]
                vs_ref[g, :NA_WIN, :LANES] = v_ref[0, rows, :]
                q_pair = q_ref[0, NA_GROUP * g:NA_GROUP * (g + 1), :]
            q = jnp.where(lo == (hh == 0), q_pair, jnp.zeros_like(q_pair))
            variant = 0 if g == 0 else (2 if g == NA_GROUPS - 1 else 1)
            scores[n] = _dot_nt(q, ks_ref[g]) + tab_ref[hh, variant]
        if n >= 1:
            g, hh = divmod(n - 1, 2)
            s = scores.pop(n - 1)
            m = jnp.max(s, axis=-1, keepdims=True)
            acc = _dot(jnp.exp2(s - m).astype(_BF), vs_ref[g])
            outs[hh] = acc[:, :LANES] / acc[:, LANES:LANES + 1]
            if hh == 1:
                o_ref[0, NA_GROUP * g:NA_GROUP * (g + 1), :] = jnp.where(lo, outs[0], outs[1]).astype(_BF)


def _na_attn(q, k, v, k_meta, v_meta, table):
    b = q.shape[0]
    seq_spec = pl.BlockSpec((1, SEQ, LANES), lambda hp, bi: (bi, 0, hp))
    meta_spec = pl.BlockSpec((N_META, LANES), lambda hp, bi: (0, hp))
    return pl.pallas_call(
        _na_kernel,
        grid=(HEADS // 2, b),
        in_specs=[seq_spec, seq_spec, seq_spec, meta_spec, meta_spec,
                  pl.BlockSpec((2, 3, NA_GROUP, NA_KEYS), lambda hp, bi: (hp, 0, 0, 0))],
        out_specs=seq_spec,
        out_shape=jax.ShapeDtypeStruct((b, SEQ, HALF_WIDTH), _BF),
        scratch_shapes=[pltpu.VMEM((NA_GROUPS, NA_KEYS, LANES), _BF),
                        pltpu.VMEM((NA_GROUPS, NA_KEYS, 2 * LANES), _BF)],
        compiler_params=pltpu.CompilerParams(dimension_semantics=("arbitrary",) * 2,
                                             vmem_limit_bytes=48 * 2 ** 20),
        name="na_attn",
    )(q, k, v, k_meta, v_meta, table)


def _na_bias_table(rel_bias):
    c = np.arange(GRID_W)
    cs = np.clip(c - NA_KW // 2, 0, GRID_W - NA_KW)
    col_ok = (c[None, :] >= cs[:, None]) & (c[None, :] < cs[:, None] + NA_KW)
    reach = GRID_W - NA_KW
    padded = jnp.pad(rel_bias * LOG2E, ((0, 0), (0, 0), (reach, reach)))
    start = NA_KW - 1 + reach
    by_col = jnp.stack([padded[:, :, start - ci:start - ci + GRID_W] for ci in range(GRID_W)], axis=2)
    by_col = jnp.where(col_ok[None, None], by_col, MASKED)
    masked_blk = jnp.full_like(by_col[:, 0], MASKED)
    tabs = []
    for g in (0, 1, NA_GROUPS - 1):
        slab_row0 = _na_slab_row(g)
        rows = []
        for i in range(NA_GROUP_ROWS):
            r = g * NA_GROUP_ROWS + i
            rs = int(np.clip(r - NA_MAX_ROWS // 2, 0, NA_ROWS - NA_MAX_ROWS))
            blks = []
            for j in range(NA_WIN_ROWS):
                kr = slab_row0 + j
                blks.append(by_col[:, kr - r + NA_MAX_ROWS - 1] if rs <= kr < rs + NA_MAX_ROWS else masked_blk)
            blks.append(jnp.zeros_like(masked_blk[:, :, :N_META]))
            blks.append(jnp.full_like(masked_blk[:, :, :NA_KEYS - NA_WIN - N_META], MASKED))
            rows.append(jnp.concatenate(blks, axis=2))
        tabs.append(jnp.concatenate(rows, axis=1))
    return jnp.stack(tabs, axis=1)


def _merge_ffn2_kernel(h1_ref, oa_ref, ob_ref, gmix_ref, wga_ref, wgb_ref, wba_ref, wbb_ref, wout_ref,
                       g2_ref, wg_ref, wu_ref, wd_ref, out_ref, act_ref):
    h1 = h1_ref[...]
    xn = _rms(h1, gmix_ref[...]).astype(_BF)
    merged = (_sigmoid(_dot(xn, wga_ref[...])) * _dot(oa_ref[...], wba_ref[...])
              + _sigmoid(_dot(xn, wgb_ref[...])) * _dot(ob_ref[...], wbb_ref[...]))
    h2 = h1 + _dot(merged.astype(_BF), wout_ref[...])
    out_ref[...] = h2 + 0.5 * _swiglu(_rms(h2, g2_ref[...]).astype(_BF), wg_ref, wu_ref, wd_ref, act_ref)


def _merge_ffn2(h1, oa, ob, p):
    n = h1.shape[0]
    row_spec = lambda w: pl.BlockSpec((TOKEN_TILE, w), lambda i: (i, 0))
    consts = [p["gmix"], p["wga"], p["wgb"], p["wba"], p["wbb"], p["wout"], p["g2"], p["wg2"], p["wu2"], p["wd2"]]
    return pl.pallas_call(
        _merge_ffn2_kernel,
        grid=(n // TOKEN_TILE,),
        in_specs=[row_spec(D_MODEL), row_spec(HALF_WIDTH), row_spec(HALF_WIDTH)]
                 + [_const_spec(c.shape) for c in consts],
        out_specs=row_spec(D_MODEL),
        out_shape=jax.ShapeDtypeStruct((n, D_MODEL), _F32),
        scratch_shapes=[pltpu.VMEM((N_FF_CHUNKS, TOKEN_TILE, FF_CHUNK), _BF)],
        compiler_params=pltpu.CompilerParams(dimension_semantics=("arbitrary",),
                                             vmem_limit_bytes=56 * 2 ** 20),
        name="merge_ffn2",
    )(h1, oa, ob, *consts)


def _rope_swapped(w):
    half = MLA_ROPE // 2
    return jnp.concatenate([w, w[..., half:], w[..., :half]], axis=-1)


def _lane_row(v, offset):
    return jnp.zeros((1, LANES), _F32).at[0, offset:offset + v.shape[0]].set(v)


def _ff_chunks(w_gate, w_up, w_down):
    cols = lambda w: w.reshape(D_MODEL, N_FF_CHUNKS, FF_CHUNK).transpose(1, 0, 2).astype(_BF)
    return cols(w_gate), cols(w_up), w_down.reshape(N_FF_CHUNKS, FF_CHUNK, D_MODEL).astype(_BF)


def _rope_tables():
    half = MLA_ROPE // 2
    pos = jnp.arange(N_META + SEQ, dtype=_F32)
    inv = ROPE_THETA ** (-jnp.arange(half, dtype=_F32) / half)
    ang = pos[:, None] * inv[None, :]
    cos, sin = jnp.cos(ang), jnp.sin(ang)
    n = pos.shape[0]
    tail = jnp.zeros((n, LANES - MLA_QK), _F32)
    c = jnp.concatenate([jnp.ones((n, MLA_NOPE), _F32), cos, cos, tail], axis=1)
    s = jnp.concatenate([jnp.zeros((n, MLA_NOPE), _F32), -sin, sin, tail], axis=1)
    return c, s


def kernel(x, meta_tokens, ffn1_norm, ffn1_w_gate, ffn1_w_up, ffn1_w_down, mix_norm, w_in, mla_q_a_norm, mla_w_uq,
           mla_kv_a_norm, mla_w_ukv, mla_q_head_norm, mla_k_head_norm, na_q_head_norm, na_k_head_norm, na_rel_bias,
           w_branch_a, w_branch_b, w_out, ffn2_norm, ffn2_w_gate, ffn2_w_up, ffn2_w_down):
    assert ffn1_norm.shape[0] == 1, "single-layer block"
    b = x.shape[0]
    row = lambda v: v.reshape(1, -1).astype(_F32)
    splits = np.cumsum((Q_LORA, KV_LORA, MLA_ROPE, HALF_WIDTH, HALF_WIDTH, HALF_WIDTH, D_MODEL, D_MODEL))
    w_cq, w_ckv, w_kr, w_qna, w_kna, w_vna, w_ga, w_gb = jnp.split(w_in[0], splits[:-1].tolist(), axis=1)
    w_kr_blk = jnp.pad(_rope_swapped(w_kr), ((0, 0), (MLA_NOPE, 0)))
    uq = mla_w_uq[0].reshape(Q_LORA, HEADS, MLA_QK)
    uq = jnp.concatenate([uq[..., :MLA_NOPE], _rope_swapped(uq[..., MLA_NOPE:])], axis=-1).reshape(Q_LORA, -1)
    kv_up = mla_w_ukv[0].reshape(KV_LORA, HEADS, 2 * MLA_NOPE)
    k_nope_up = jnp.pad(kv_up[..., :MLA_NOPE], ((0, 0), (0, 0), (0, LANES - MLA_NOPE))).reshape(KV_LORA, -1)
    gq_mla = mla_q_head_norm[0] * (MLA_QK ** -0.5 * LOG2E)
    gk_mla = mla_k_head_norm[0]
    gq_na = na_q_head_norm[0] * (NA_HEAD_DIM ** -0.5 * LOG2E)
    p = {
        "g1": row(ffn1_norm[0]), "gmix": row(mix_norm[0]), "g2": row(ffn2_norm[0]),
        "win_r": jnp.concatenate([w_cq, w_ckv, w_kr_blk, w_qna, w_kna, w_vna], axis=1).astype(_BF),
        "qag": row(mla_q_a_norm[0]), "kvag": row(mla_kv_a_norm[0]),
        "wuq": uq.astype(_BF),
        "wukv": jnp.concatenate([k_nope_up, kv_up[..., MLA_NOPE:].reshape(KV_LORA, -1)], axis=1).astype(_BF),
        "gq": row(jnp.concatenate([gq_mla[:MLA_NOPE], _rope_swapped(gq_mla[MLA_NOPE:])])),
        "gkn": _lane_row(gk_mla[:MLA_NOPE], 0),
        "gkr": _lane_row(_rope_swapped(gk_mla[MLA_NOPE:]), MLA_NOPE),
        "gqna": row(jnp.concatenate([gq_na, gq_na])),
        "gkna": row(jnp.concatenate([na_k_head_norm[0], na_k_head_norm[0]])),
        "wga": w_ga.astype(_BF), "wgb": w_gb.astype(_BF),
        "wba": w_branch_a[0].astype(_BF), "wbb": w_branch_b[0].astype(_BF), "wout": w_out[0].astype(_BF),
    }
    p["wg1"], p["wu1"], p["wd1"] = _ff_chunks(ffn1_w_gate[0], ffn1_w_up[0], ffn1_w_down[0])
    p["wg2"], p["wu2"], p["wd2"] = _ff_chunks(ffn2_w_gate[0], ffn2_w_up[0], ffn2_w_down[0])
    assert p["win_r"].shape[1] == _IN_R_WIDTH and p["wuq"].shape[1] == HEADS * LANES

    tabs = _rope_tables()
    meta = _ffn1_proj(meta_tokens.astype(_F32), [t[:N_META] for t in tabs], p, N_META, 40)
    _, _, km_meta, vm_meta, _, kn_meta, vn_meta = meta
    real = _ffn1_proj(x.reshape(b * SEQ, D_MODEL), [t[N_META:] for t in tabs], p, TOKEN_TILE, 56)
    h1, qm, km, vm, qn, kn, vn = real
    seq = lambda a: a.reshape(b, SEQ, a.shape[-1])
    o_a = _mla_attn(seq(qm), seq(km), seq(vm), km_meta, vm_meta)
    o_b = _na_attn(seq(qn), seq(kn), seq(vn), kn_meta, vn_meta, _na_bias_table(na_rel_bias[0]))
    out = _merge_ffn2(h1, o_a.reshape(b * SEQ, HALF_WIDTH), o_b.reshape(b * SEQ, HALF_WIDTH), p)
    return out.reshape(b, SEQ, D_MODEL)
```

```python
import math

import jax
import jax.numpy as jnp
import numpy as np
from jax import lax
from jax.experimental import pallas as pl
from jax.experimental.pallas import tpu as pltpu

D_MODEL = 1024
SEQ = 4096
N_META = 16
GRID_W = 64
HEADS = 8
MLA_NOPE = 64
MLA_ROPE = 32
MLA_QK = MLA_NOPE + MLA_ROPE
Q_LORA = 384
KV_LORA = 256
NA_HEAD_DIM = 64
NA_MAX_ROWS = 8
NA_KW = 16
D_FF = 2816
ROPE_THETA = 10000.0
EPS = 1e-6
HALF_WIDTH = 512
LOG2E = math.log2(math.e)

LANES = 128
HALF_LANES = LANES // 2
FF_CHUNK = 256
N_FF_CHUNKS = D_FF // FF_CHUNK
assert N_FF_CHUNKS * FF_CHUNK == D_FF

TOKEN_TILE = 512
MLA_Q_TILE = 256
MLA_K_TILE = 256
MLA_GROUP_TILES = 8
MLA_PV_LAG = 12
NA_GROUP_ROWS = 4
NA_GROUP = NA_GROUP_ROWS * GRID_W
NA_WIN_ROWS = NA_GROUP_ROWS + NA_MAX_ROWS - 1
NA_WIN = NA_WIN_ROWS * GRID_W
NA_KEYS = 768
assert NA_WIN + N_META <= NA_KEYS and NA_KEYS % 256 == 0
NA_ROWS = SEQ // GRID_W
NA_GROUPS = NA_ROWS // NA_GROUP_ROWS
MASKED = -1e30

_O_CQ, _O_CKV, _O_KR = 0, Q_LORA, Q_LORA + KV_LORA
_O_QNA = _O_KR + LANES
_O_KNA = _O_QNA + HALF_WIDTH
_O_VNA = _O_KNA + HALF_WIDTH
_IN_R_WIDTH = _O_VNA + HALF_WIDTH

_BF = jnp.bfloat16
_F32 = jnp.float32


def _dot(a, b):
    return jnp.dot(a, b, preferred_element_type=_F32)


def _dot_nt(a, b):
    return lax.dot_general(a, b, (((1,), (1,)), ((), ())), preferred_element_type=_F32)


def _sigmoid(x):
    return 1.0 / (1.0 + jnp.exp(-x))


def _rms(x, g):
    return x * lax.rsqrt(jnp.mean(x * x, axis=-1, keepdims=True) + EPS) * g


def _lane_iota(rows):
    return lax.broadcasted_iota(jnp.int32, (rows, LANES), 1)


def _swiglu_hidden(xn, wg_ref, wu_ref, act_ref, side_work=()):
    side_work = list(side_work)
    per_chunk = -(-len(side_work) // N_FF_CHUNKS)
    for c in range(N_FF_CHUNKS):
        g = _dot(xn, wg_ref[c])
        u = _dot(xn, wu_ref[c])
        act_ref[c] = (g * _sigmoid(g) * u).astype(_BF)
        for piece in side_work[c * per_chunk:(c + 1) * per_chunk]:
            piece()


def _swiglu_down(act_ref, wd_ref):
    y = _dot(act_ref[0], wd_ref[0])
    for c in range(1, N_FF_CHUNKS):
        y += _dot(act_ref[c], wd_ref[c])
    return y


def _rope(x, c, s):
    return x * c + pltpu.roll(x, LANES - MLA_ROPE, 1) * s


def _half_head_norm(blk, g, lo):
    sq = blk * blk
    s_lo = jnp.sum(jnp.where(lo, sq, 0.0), axis=-1, keepdims=True)
    s_hi = jnp.sum(jnp.where(lo, 0.0, sq), axis=-1, keepdims=True)
    r = lax.rsqrt(jnp.where(lo, s_lo, s_hi) * (1.0 / NA_HEAD_DIM) + EPS)
    return (blk * r * g).astype(_BF)


def _ffn1_proj_kernel(x_ref, g1_ref, wg_ref, wu_ref, wd_ref, gmix_ref, win_ref, qag_ref, wuq_ref,
                      kvag_ref, wukv_ref, gq_ref, gkn_ref, gkr_ref, gqna_ref, gkna_ref, rc_ref, rs_ref,
                      h1_ref, qm_ref, km_ref, vm_ref, qn_ref, kn_ref, vn_ref,
                      act_ref, raw_q_ref, raw_k_ref, raw_kr_ref, raw_qn_ref, raw_kn_ref):
    step = pl.program_id(0)
    last = pl.num_programs(0) - 1
    rows = x_ref.shape[0]
    raw_refs = (raw_q_ref, raw_k_ref, raw_kr_ref, raw_qn_ref, raw_kn_ref)

    def head_work():
        qk_lanes = _lane_iota(rows) < MLA_QK
        lo = _lane_iota(rows) < HALF_LANES
        shared = {}

        def q_head(h):
            blk = slice(LANES * h, LANES * (h + 1))
            qh = raw_q_ref[:, blk]
            ss = jnp.sum(jnp.where(qk_lanes, qh * qh, 0.0), axis=-1, keepdims=True)
            r = lax.rsqrt(ss * (1.0 / MLA_QK) + EPS)
            qm_ref[:, blk] = _rope(qh * r * gq_ref[...], rc_ref[...], rs_ref[...]).astype(_BF)

        def k_rope():
            kr = raw_kr_ref[...]
            shared["ss_r"] = jnp.sum(jnp.where(qk_lanes, kr * kr, 0.0), axis=-1, keepdims=True)
            shared["krr"] = _rope(kr * gkr_ref[...], rc_ref[...], rs_ref[...])

        def k_head(h):
            blk = slice(LANES * h, LANES * (h + 1))
            kh = raw_k_ref[:, blk]
            ss = jnp.sum(kh * kh, axis=-1, keepdims=True) + shared["ss_r"]
            r = lax.rsqrt(ss * (1.0 / MLA_QK) + EPS)
            km_ref[:, blk] = ((kh * gkn_ref[...] + shared["krr"]) * r).astype(_BF)

        def na_block(raw_ref, g_ref, out_ref, j):
            blk = slice(LANES * j, LANES * (j + 1))
            out_ref[:, blk] = _half_head_norm(raw_ref[:, blk], g_ref[...], lo)

        pieces = [lambda h=h: q_head(h) for h in range(HEADS)] + [k_rope]
        pieces += [lambda h=h: k_head(h) for h in range(HEADS)]
        for raw_ref, g_ref, out_ref in ((raw_qn_ref, gqna_ref, qn_ref), (raw_kn_ref, gkna_ref, kn_ref)):
            pieces += [lambda a=(raw_ref, g_ref, out_ref, j): na_block(*a) for j in range(HALF_WIDTH // LANES)]
        return pieces

    @pl.when(step == 0)
    def _():
        for ref in raw_refs:
            ref[...] = jnp.zeros_like(ref)

    @pl.when(step < last)
    def _():
        xn = _rms(x_ref[...], g1_ref[...]).astype(_BF)
        _swiglu_hidden(xn, wg_ref, wu_ref, act_ref, side_work=head_work())

    @pl.when(step + 1 <= last)
    def _():
        h1 = x_ref[...] + 0.5 * _swiglu_down(act_ref, wd_ref)
        h1_ref[...] = h1
        xn = _rms(h1, gmix_ref[...]).astype(_BF)
        cq = _dot(xn, win_ref[:, _O_CQ:_O_CQ + Q_LORA])
        raw_q_ref[...] = _dot(_rms(cq, qag_ref[...]).astype(_BF), wuq_ref[...])
        ckv = _dot(xn, win_ref[:, _O_CKV:_O_CKV + KV_LORA])
        kvu = _dot(_rms(ckv, kvag_ref[...]).astype(_BF), wukv_ref[...])
        raw_k_ref[...] = kvu[:, :HEADS * LANES]
        vm_ref[...] = kvu[:, HEADS * LANES:].astype(_BF)
        raw_kr_ref[...] = _dot(xn, win_ref[:, _O_KR:_O_KR + LANES])
        raw_qn_ref[...] = _dot(xn, win_ref[:, _O_QNA:_O_QNA + HALF_WIDTH])
        raw_kn_ref[...] = _dot(xn, win_ref[:, _O_KNA:_O_KNA + HALF_WIDTH])
        vn_ref[...] = _dot(xn, win_ref[:, _O_VNA:_O_VNA + HALF_WIDTH]).astype(_BF)

    @pl.when(step == last)
    def _():
        for piece in head_work():
            piece()


def _const_spec(shape):
    nd = len(shape)
    return pl.BlockSpec(shape, lambda i: (0,) * nd, pipeline_mode=pl.Buffered(1))


def _ffn1_proj(x2d, rope_tabs, p, tile, vmem_mb):
    n_tiles = x2d.shape[0] // tile
    tabs_per_seq = rope_tabs[0].shape[0] // tile
    cur = lambda i: jnp.minimum(i, n_tiles - 1)
    prev = lambda i: jnp.maximum(i - 1, 0)
    cur_spec = lambda w: pl.BlockSpec((tile, w), lambda i: (cur(i), 0))
    prev_spec = lambda w: pl.BlockSpec((tile, w), lambda i: (prev(i), 0))
    tab_spec = pl.BlockSpec((tile, LANES), lambda i: (prev(i) % tabs_per_seq, 0))
    consts = [p["g1"], p["wg1"], p["wu1"], p["wd1"], p["gmix"], p["win_r"], p["qag"], p["wuq"],
              p["kvag"], p["wukv"], p["gq"], p["gkn"], p["gkr"], p["gqna"], p["gkna"]]
    wide = HEADS * LANES
    widths = [wide, wide, HALF_WIDTH, HALF_WIDTH, HALF_WIDTH, HALF_WIDTH]
    out_specs = [cur_spec(D_MODEL), prev_spec(wide), prev_spec(wide), cur_spec(HALF_WIDTH),
                 prev_spec(HALF_WIDTH), prev_spec(HALF_WIDTH), cur_spec(HALF_WIDTH)]
    out_shape = [jax.ShapeDtypeStruct((x2d.shape[0], D_MODEL), _F32)]
    out_shape += [jax.ShapeDtypeStruct((x2d.shape[0], w), _BF) for w in widths]
    raw = lambda w: pltpu.VMEM((tile, w), _F32)
    return pl.pallas_call(
        _ffn1_proj_kernel,
        grid=(n_tiles + 1,),
        in_specs=[cur_spec(D_MODEL)] + [_const_spec(c.shape) for c in consts] + [tab_spec] * 2,
        out_specs=out_specs,
        out_shape=out_shape,
        scratch_shapes=[pltpu.VMEM((N_FF_CHUNKS, tile, FF_CHUNK), _BF),
                        raw(wide), raw(wide), raw(LANES), raw(HALF_WIDTH), raw(HALF_WIDTH)],
        compiler_params=pltpu.CompilerParams(dimension_semantics=("arbitrary",),
                                             vmem_limit_bytes=vmem_mb * 2 ** 20),
        name="ffn1_proj",
    )(x2d, *consts, *rope_tabs)


def _extend_pair(v_pair, dst_ref):
    dst_ref[:, :LANES] = v_pair
    dst_ref[:, LANES:] = jnp.ones_like(v_pair)


def _mla_kernel(q_ref, k_ref, v_ref, km_ref, vm_ref, o_ref, vext_ref, vmext_ref):
    @pl.when(pl.program_id(1) == 0)
    def _():
        for hp in range(HEADS // 2):
            _extend_pair(v_ref[0, :, LANES * hp:LANES * (hp + 1)], vext_ref.at[hp])
            _extend_pair(vm_ref[:, LANES * hp:LANES * (hp + 1)], vmext_ref.at[hp])

    n_tiles = SEQ // MLA_K_TILE
    n_groups = n_tiles // MLA_GROUP_TILES
    slots = HEADS * n_tiles
    scores, meta_scores, group_max, parts, outs = {}, {}, {}, {}, {}
    run_max = None
    for g in range(slots + MLA_PV_LAG):
        if g < slots:
            h, t = divmod(g, n_tiles)
            sl = slice(LANES * h, LANES * (h + 1))
            if t == 0:
                q = q_ref[0, :, sl]
                meta_scores[h] = _dot_nt(q, km_ref[:, sl])
            s = _dot_nt(q, k_ref[0, MLA_K_TILE * t:MLA_K_TILE * (t + 1), sl])
            run_max = s if t % MLA_GROUP_TILES == 0 else jnp.maximum(run_max, s)
            scores[(h, t)] = s
            if t % MLA_GROUP_TILES == MLA_GROUP_TILES - 1:
                m = jnp.max(run_max, axis=-1, keepdims=True)
                if t // MLA_GROUP_TILES == 0:
                    m = jnp.maximum(m, jnp.max(meta_scores[h], axis=-1, keepdims=True))
                group_max[(h, t // MLA_GROUP_TILES)] = m
        gp = g - MLA_PV_LAG
        if gp >= 0:
            h, t = divmod(gp, n_tiles)
            j = t // MLA_GROUP_TILES
            m = group_max[(h, j)]
            part = _dot(jnp.exp2(scores.pop((h, t)) - m).astype(_BF),
                        vext_ref[h // 2, MLA_K_TILE * t:MLA_K_TILE * (t + 1), :])
            if t == 0:
                part += _dot(jnp.exp2(meta_scores[h] - m).astype(_BF), vmext_ref[h // 2])
            parts[(h, j)] = part if t % MLA_GROUP_TILES == 0 else parts[(h, j)] + part
            if t == n_tiles - 1:
                m_all = group_max[(h, 0)]
                for jj in range(1, n_groups):
                    m_all = jnp.maximum(m_all, group_max[(h, jj)])
                acc = sum(jnp.exp2(group_max[(h, jj)] - m_all) * parts.pop((h, jj)) for jj in range(n_groups))
                outs[h] = acc[:, :LANES] / acc[:, LANES:LANES + 1]
    lo = _lane_iota(MLA_Q_TILE) < HALF_LANES
    for hp in range(HEADS // 2):
        o_ref[0, :, LANES * hp:LANES * (hp + 1)] = jnp.where(lo, outs[2 * hp], outs[2 * hp + 1]).astype(_BF)


def _mla_attn(q, k, v, k_meta, v_meta):
    b = q.shape[0]
    once = pl.Buffered(1)
    whole = lambda shape: pl.BlockSpec(shape, lambda bi, i: (0,) * len(shape), pipeline_mode=once)
    return pl.pallas_call(
        _mla_kernel,
        grid=(b, SEQ // MLA_Q_TILE),
        in_specs=[pl.BlockSpec((1, MLA_Q_TILE, HEADS * LANES), lambda bi, i: (bi, i, 0)),
                  pl.BlockSpec((1, SEQ, HEADS * LANES), lambda bi, i: (bi, 0, 0), pipeline_mode=once),
                  pl.BlockSpec((1, SEQ, HALF_WIDTH), lambda bi, i: (bi, 0, 0), pipeline_mode=once),
                  whole((N_META, HEADS * LANES)), whole((N_META, HALF_WIDTH))],
        out_specs=pl.BlockSpec((1, MLA_Q_TILE, HALF_WIDTH), lambda bi, i: (bi, i, 0)),
        out_shape=jax.ShapeDtypeStruct((b, SEQ, HALF_WIDTH), _BF),
        scratch_shapes=[pltpu.VMEM((HEADS // 2, SEQ, 2 * LANES), _BF),
                        pltpu.VMEM((HEADS // 2, N_META, 2 * LANES), _BF)],
        compiler_params=pltpu.CompilerParams(dimension_semantics=("arbitrary",) * 2,
                                             vmem_limit_bytes=56 * 2 ** 20),
        name="mla_attn",
    )(q, k, v, k_meta, v_meta)


def _na_slab_row(g):
    return int(np.clip(g * NA_GROUP_ROWS - NA_MAX_ROWS // 2, 0, NA_ROWS - NA_WIN_ROWS))


def _na_kernel(q_ref, k_ref, v_ref, km_ref, vm_ref, tab_ref, o_ref, ks_ref, vs_ref):
    @pl.when(pl.program_id(1) == 0)
    def _():
        for buf in range(NA_GROUPS):
            ks_ref[buf, NA_WIN:, :] = jnp.zeros((NA_KEYS - NA_WIN, LANES), _BF)
            vs_ref[buf, NA_WIN:, :] = jnp.zeros((NA_KEYS - NA_WIN, 2 * LANES), _BF)
            ks_ref[buf, NA_WIN:NA_WIN + N_META, :] = km_ref[...]
            vs_ref[buf, NA_WIN:NA_WIN + N_META, :LANES] = vm_ref[...]
            vs_ref[buf, :NA_WIN + N_META, LANES:] = jnp.ones((NA_WIN + N_META, LANES), _BF)

    lo = _lane_iota(NA_GROUP) < HALF_LANES

    scores, outs = {}, {}
    items = 2 * NA_GROUPS
    for n in range(items + 1):
        if n < items:
            g, hh = divmod(n, 2)
            if hh == 0:
                rows = slice(_na_slab_row(g) * GRID_W, _na_slab_row(g) * GRID_W + NA_WIN)
                ks_ref[g, :NA_WIN, :] = k_ref[0, rows, :]
                vs_ref[g, :NA_WIN, :LANES] = v_ref[0, rows, :]
                q_pair = q_ref[0, NA_GROUP * g:NA_GROUP * (g + 1), :]
            q = jnp.where(lo == (hh == 0), q_pair, jnp.zeros_like(q_pair))
            variant = 0 if g == 0 else (2 if g == NA_GROUPS - 1 else 1)
            scores[n] = _dot_nt(q, ks_ref[g]) + tab_ref[hh, variant]
        if n >= 1:
            g, hh = divmod(n - 1, 2)
            s = scores.pop(n - 1)
            m = jnp.max(s, axis=-1, keepdims=True)
            acc = _dot(jnp.exp2(s - m).astype(_BF), vs_ref[g])
            outs[hh] = acc[:, :LANES] / acc[:, LANES:LANES + 1]
            if hh == 1:
                o_ref[0, NA_GROUP * g:NA_GROUP * (g + 1), :] = jnp.where(lo, outs[0], outs[1]).astype(_BF)


def _na_attn(q, k, v, k_meta, v_meta, table):
    b = q.shape[0]
    seq_spec = pl.BlockSpec((1, SEQ, LANES), lambda hp, bi: (bi, 0, hp))
    meta_spec = pl.BlockSpec((N_META, LANES), lambda hp, bi: (0, hp))
    return pl.pallas_call(
        _na_kernel,
        grid=(HEADS // 2, b),
        in_specs=[seq_spec, seq_spec, seq_spec, meta_spec, meta_spec,
                  pl.BlockSpec((2, 3, NA_GROUP, NA_KEYS), lambda hp, bi: (hp, 0, 0, 0))],
        out_specs=seq_spec,
        out_shape=jax.ShapeDtypeStruct((b, SEQ, HALF_WIDTH), _BF),
        scratch_shapes=[pltpu.VMEM((NA_GROUPS, NA_KEYS, LANES), _BF),
                        pltpu.VMEM((NA_GROUPS, NA_KEYS, 2 * LANES), _BF)],
        compiler_params=pltpu.CompilerParams(dimension_semantics=("arbitrary",) * 2,
                                             vmem_limit_bytes=48 * 2 ** 20),
        name="na_attn",
    )(q, k, v, k_meta, v_meta, table)


def _na_bias_table(rel_bias):
    n_dr = 2 * NA_MAX_ROWS - 1
    c = np.arange(GRID_W)
    cs = np.clip(c - NA_KW // 2, 0, GRID_W - NA_KW)
    col_ok = (c[None, :] >= cs[:, None]) & (c[None, :] < cs[:, None] + NA_KW)
    reach = GRID_W - NA_KW
    line = jnp.pad(rel_bias * LOG2E, ((0, 0), (0, 0), (reach, reach)))
    assert line.shape[-1] == 2 * GRID_W - 1
    skew = jnp.pad(jnp.tile(line, (1, 1, GRID_W)), ((0, 0), (0, 0), (0, GRID_W)))
    skew = skew.reshape(HEADS, n_dr, GRID_W, 2 * GRID_W)[:, :, ::-1, :GRID_W]
    by_col = jnp.where(col_ok[None, None], skew, MASKED)
    by_col = jnp.concatenate([by_col, jnp.full_like(by_col[:, :1], MASKED)], axis=1)
    pick = np.full((3, NA_GROUP_ROWS, NA_WIN_ROWS), n_dr)
    for v, g in enumerate((0, 1, NA_GROUPS - 1)):
        for i in range(NA_GROUP_ROWS):
            r = g * NA_GROUP_ROWS + i
            rs = int(np.clip(r - NA_MAX_ROWS // 2, 0, NA_ROWS - NA_MAX_ROWS))
            for j in range(NA_WIN_ROWS):
                kr = _na_slab_row(g) + j
                if rs <= kr < rs + NA_MAX_ROWS:
                    pick[v, i, j] = kr - r + NA_MAX_ROWS - 1
    onehot = jnp.asarray(np.eye(n_dr + 1, dtype=np.float32)[pick.reshape(-1)])
    blocks = jnp.einsum("pd,hdcx->hpcx", onehot, by_col, precision=lax.Precision.HIGHEST)
    win = blocks.reshape(HEADS, 3, NA_GROUP_ROWS, NA_WIN_ROWS, GRID_W, GRID_W)
    win = win.transpose(0, 1, 2, 4, 3, 5).reshape(HEADS, 3, NA_GROUP, NA_WIN)
    tail = jnp.concatenate([jnp.zeros((N_META,), _F32), jnp.full((NA_KEYS - NA_WIN - N_META,), MASKED, _F32)])
    return jnp.concatenate([win, jnp.broadcast_to(tail, (HEADS, 3, NA_GROUP, tail.shape[0]))], axis=-1)


def _merge_ffn2_kernel(h1_ref, oa_ref, ob_ref, gmix_ref, wga_ref, wgb_ref, wba_ref, wbb_ref, wout_ref,
                       g2_ref, wg_ref, wu_ref, wd_ref, out_ref, act_ref):
    h1 = h1_ref[...]
    xn = _rms(h1, gmix_ref[...]).astype(_BF)
    merged = (_sigmoid(_dot(xn, wga_ref[...])) * _dot(oa_ref[...], wba_ref[...])
              + _sigmoid(_dot(xn, wgb_ref[...])) * _dot(ob_ref[...], wbb_ref[...]))
    h2 = h1 + _dot(merged.astype(_BF), wout_ref[...])
    _swiglu_hidden(_rms(h2, g2_ref[...]).astype(_BF), wg_ref, wu_ref, act_ref)
    out_ref[...] = h2 + 0.5 * _swiglu_down(act_ref, wd_ref)


def _merge_ffn2(h1, oa, ob, p):
    n = h1.shape[0]
    row_spec = lambda w: pl.BlockSpec((TOKEN_TILE, w), lambda i: (i, 0))
    consts = [p["gmix"], p["wga"], p["wgb"], p["wba"], p["wbb"], p["wout"], p["g2"], p["wg2"], p["wu2"], p["wd2"]]
    return pl.pallas_call(
        _merge_ffn2_kernel,
        grid=(n // TOKEN_TILE,),
        in_specs=[row_spec(D_MODEL), row_spec(HALF_WIDTH), row_spec(HALF_WIDTH)]
                 + [_const_spec(c.shape) for c in consts],
        out_specs=row_spec(D_MODEL),
        out_shape=jax.ShapeDtypeStruct((n, D_MODEL), _F32),
        scratch_shapes=[pltpu.VMEM((N_FF_CHUNKS, TOKEN_TILE, FF_CHUNK), _BF)],
        compiler_params=pltpu.CompilerParams(dimension_semantics=("arbitrary",),
                                             vmem_limit_bytes=56 * 2 ** 20),
        name="merge_ffn2",
    )(h1, oa, ob, *consts)


def _rope_swapped(w):
    half = MLA_ROPE // 2
    return jnp.concatenate([w, w[..., half:], w[..., :half]], axis=-1)


def _lane_row(v, offset):
    return jnp.zeros((1, LANES), _F32).at[0, offset:offset + v.shape[0]].set(v)


def _ff_chunks(w_gate, w_up, w_down):
    cols = lambda w: w.reshape(D_MODEL, N_FF_CHUNKS, FF_CHUNK).transpose(1, 0, 2).astype(_BF)
    return cols(w_gate), cols(w_up), w_down.reshape(N_FF_CHUNKS, FF_CHUNK, D_MODEL).astype(_BF)


def _rope_tables():
    half = MLA_ROPE // 2
    pos = jnp.arange(N_META + SEQ, dtype=_F32)
    inv = ROPE_THETA ** (-jnp.arange(half, dtype=_F32) / half)
    ang = pos[:, None] * inv[None, :]
    cos, sin = jnp.cos(ang), jnp.sin(ang)
    n = pos.shape[0]
    tail = jnp.zeros((n, LANES - MLA_QK), _F32)
    c = jnp.concatenate([jnp.ones((n, MLA_NOPE), _F32), cos, cos, tail], axis=1)
    s = jnp.concatenate([jnp.zeros((n, MLA_NOPE), _F32), -sin, sin, tail], axis=1)
    return c, s


def kernel(x, meta_tokens, ffn1_norm, ffn1_w_gate, ffn1_w_up, ffn1_w_down, mix_norm, w_in, mla_q_a_norm, mla_w_uq,
           mla_kv_a_norm, mla_w_ukv, mla_q_head_norm, mla_k_head_norm, na_q_head_norm, na_k_head_norm, na_rel_bias,
           w_branch_a, w_branch_b, w_out, ffn2_norm, ffn2_w_gate, ffn2_w_up, ffn2_w_down):
    assert ffn1_norm.shape[0] == 1, "single-layer block"
    b = x.shape[0]
    row = lambda v: v.reshape(1, -1).astype(_F32)
    splits = np.cumsum((Q_LORA, KV_LORA, MLA_ROPE, HALF_WIDTH, HALF_WIDTH, HALF_WIDTH, D_MODEL, D_MODEL))
    w_cq, w_ckv, w_kr, w_qna, w_kna, w_vna, w_ga, w_gb = jnp.split(w_in[0], splits[:-1].tolist(), axis=1)
    w_kr_blk = jnp.pad(_rope_swapped(w_kr), ((0, 0), (MLA_NOPE, 0)))
    uq = mla_w_uq[0].reshape(Q_LORA, HEADS, MLA_QK)
    uq = jnp.concatenate([uq[..., :MLA_NOPE], _rope_swapped(uq[..., MLA_NOPE:])], axis=-1).reshape(Q_LORA, -1)
    kv_up = mla_w_ukv[0].reshape(KV_LORA, HEADS, 2 * MLA_NOPE)
    k_nope_up = jnp.pad(kv_up[..., :MLA_NOPE], ((0, 0), (0, 0), (0, LANES - MLA_NOPE))).reshape(KV_LORA, -1)
    gq_mla = mla_q_head_norm[0] * (MLA_QK ** -0.5 * LOG2E)
    gk_mla = mla_k_head_norm[0]
    gq_na = na_q_head_norm[0] * (NA_HEAD_DIM ** -0.5 * LOG2E)
    p = {
        "g1": row(ffn1_norm[0]), "gmix": row(mix_norm[0]), "g2": row(ffn2_norm[0]),
        "win_r": jnp.concatenate([w_cq, w_ckv, w_kr_blk, w_qna, w_kna, w_vna], axis=1).astype(_BF),
        "qag": row(mla_q_a_norm[0]), "kvag": row(mla_kv_a_norm[0]),
        "wuq": uq.astype(_BF),
        "wukv": jnp.concatenate([k_nope_up, kv_up[..., MLA_NOPE:].reshape(KV_LORA, -1)], axis=1).astype(_BF),
        "gq": row(jnp.concatenate([gq_mla[:MLA_NOPE], _rope_swapped(gq_mla[MLA_NOPE:])])),
        "gkn": _lane_row(gk_mla[:MLA_NOPE], 0),
        "gkr": _lane_row(_rope_swapped(gk_mla[MLA_NOPE:]), MLA_NOPE),
        "gqna": row(jnp.concatenate([gq_na, gq_na])),
        "gkna": row(jnp.concatenate([na_k_head_norm[0], na_k_head_norm[0]])),
        "wga": w_ga.astype(_BF), "wgb": w_gb.astype(_BF),
        "wba": w_branch_a[0].astype(_BF), "wbb": w_branch_b[0].astype(_BF), "wout": w_out[0].astype(_BF),
    }
    p["wg1"], p["wu1"], p["wd1"] = _ff_chunks(ffn1_w_gate[0], ffn1_w_up[0], ffn1_w_down[0])
    p["wg2"], p["wu2"], p["wd2"] = _ff_chunks(ffn2_w_gate[0], ffn2_w_up[0], ffn2_w_down[0])
    assert p["win_r"].shape[1] == _IN_R_WIDTH and p["wuq"].shape[1] == HEADS * LANES

    tabs = _rope_tables()
    meta = _ffn1_proj(meta_tokens.astype(_F32), [t[:N_META] for t in tabs], p, N_META, 40)
    _, _, km_meta, vm_meta, _, kn_meta, vn_meta = meta
    real = _ffn1_proj(x.reshape(b * SEQ, D_MODEL), [t[N_META:] for t in tabs], p, TOKEN_TILE, 56)
    h1, qm, km, vm, qn, kn, vn = real
    seq = lambda a: a.reshape(b, SEQ, a.shape[-1])
    o_a = _mla_attn(seq(qm), seq(km), seq(vm), km_meta, vm_meta)
    o_b = _na_attn(seq(qn), seq(kn), seq(vn), kn_meta, vn_meta, _na_bias_table(na_rel_bias[0]))
    out = _merge_ffn2(h1, o_a.reshape(b * SEQ, HALF_WIDTH), o_b.reshape(b * SEQ, HALF_WIDTH), p)
    return out.reshape(b, SEQ, D_MODEL)
```

```python
import math

import jax
import jax.numpy as jnp
import numpy as np
from jax import lax
from jax.experimental import pallas as pl
from jax.experimental.pallas import tpu as pltpu

D_MODEL = 1024
SEQ = 4096
N_META = 16
GRID_W = 64
HEADS = 8
MLA_NOPE = 64
MLA_ROPE = 32
MLA_QK = MLA_NOPE + MLA_ROPE
Q_LORA = 384
KV_LORA = 256
NA_HEAD_DIM = 64
NA_MAX_ROWS = 8
NA_KW = 16
D_FF = 2816
ROPE_THETA = 10000.0
EPS = 1e-6
HALF_WIDTH = 512
LOG2E = math.log2(math.e)

LANES = 128
HALF_LANES = LANES // 2
FF_CHUNK = 256
N_FF_CHUNKS = D_FF // FF_CHUNK
assert N_FF_CHUNKS * FF_CHUNK == D_FF

TOKEN_TILE = 512
MLA_Q_TILE = 256
MLA_K_TILE = 256
MLA_GROUP_TILES = 8
MLA_PV_LAG = 12
NA_GROUP_ROWS = 4
NA_GROUP = NA_GROUP_ROWS * GRID_W
NA_WIN_ROWS = NA_GROUP_ROWS + NA_MAX_ROWS - 1
NA_WIN = NA_WIN_ROWS * GRID_W
NA_KEYS = 768
assert NA_WIN + N_META <= NA_KEYS and NA_KEYS % 256 == 0
NA_ROWS = SEQ // GRID_W
NA_GROUPS = NA_ROWS // NA_GROUP_ROWS
MASKED = -1e30

MIB = 2 ** 20
V7X_VMEM_MIB = 64
VMEM_MIB = {"ffn1_proj": 56, "ffn1_proj_meta": 40, "mla_attn": 56, "na_attn": 48, "merge_ffn2": 56}
assert max(VMEM_MIB.values()) < V7X_VMEM_MIB

_O_CQ, _O_CKV, _O_KR = 0, Q_LORA, Q_LORA + KV_LORA
_O_QNA = _O_KR + LANES
_O_KNA = _O_QNA + HALF_WIDTH
_O_VNA = _O_KNA + HALF_WIDTH
_IN_R_WIDTH = _O_VNA + HALF_WIDTH

_BF = jnp.bfloat16
_F32 = jnp.float32


def _dot(a, b):
    return jnp.dot(a, b, preferred_element_type=_F32)


def _dot_nt(a, b):
    return lax.dot_general(a, b, (((1,), (1,)), ((), ())), preferred_element_type=_F32)


def _sigmoid(x):
    return 1.0 / (1.0 + jnp.exp(-x))


def _rms(x, g):
    return x * lax.rsqrt(jnp.mean(x * x, axis=-1, keepdims=True) + EPS) * g


def _lane_iota(rows):
    return lax.broadcasted_iota(jnp.int32, (rows, LANES), 1)


def _swiglu_hidden(xn, wg_ref, wu_ref, act_ref, side_work=()):
    side_work = list(side_work)
    per_chunk = -(-len(side_work) // N_FF_CHUNKS)
    for c in range(N_FF_CHUNKS):
        cols = slice(FF_CHUNK * c, FF_CHUNK * (c + 1))
        g = _dot(xn, wg_ref[:, cols])
        u = _dot(xn, wu_ref[:, cols])
        act_ref[c] = (g * _sigmoid(g) * u).astype(_BF)
        for piece in side_work[c * per_chunk:(c + 1) * per_chunk]:
            piece()


def _swiglu_down(act_ref, wd_ref):
    y = _dot(act_ref[0], wd_ref[0])
    for c in range(1, N_FF_CHUNKS):
        y += _dot(act_ref[c], wd_ref[c])
    return y


def _rope(x, c, s):
    return x * c + pltpu.roll(x, LANES - MLA_ROPE, 1) * s


def _half_head_norm(blk, g, lo):
    sq = blk * blk
    s_lo = jnp.sum(jnp.where(lo, sq, 0.0), axis=-1, keepdims=True)
    s_hi = jnp.sum(jnp.where(lo, 0.0, sq), axis=-1, keepdims=True)
    r = lax.rsqrt(jnp.where(lo, s_lo, s_hi) * (1.0 / NA_HEAD_DIM) + EPS)
    return (blk * r * g).astype(_BF)


def _ffn1_proj_kernel(x_ref, g1_ref, wg_ref, wu_ref, wd_ref, gmix_ref, win_ref, qag_ref, wuq_ref,
                      kvag_ref, wukv_ref, gq_ref, gkn_ref, gkr_ref, gqna_ref, gkna_ref, rc_ref, rs_ref,
                      h1_ref, qm_ref, km_ref, vm_ref, qn_ref, kn_ref, vn_ref,
                      act_ref, raw_q_ref, raw_k_ref, raw_kr_ref, raw_qn_ref, raw_kn_ref):
    step = pl.program_id(0)
    last = pl.num_programs(0) - 1
    rows = x_ref.shape[0]
    raw_refs = (raw_q_ref, raw_k_ref, raw_kr_ref, raw_qn_ref, raw_kn_ref)

    def head_work():
        qk_lanes = _lane_iota(rows) < MLA_QK
        lo = _lane_iota(rows) < HALF_LANES
        shared = {}

        def q_head(h):
            blk = slice(LANES * h, LANES * (h + 1))
            qh = raw_q_ref[:, blk]
            ss = jnp.sum(jnp.where(qk_lanes, qh * qh, 0.0), axis=-1, keepdims=True)
            r = lax.rsqrt(ss * (1.0 / MLA_QK) + EPS)
            qm_ref[:, blk] = _rope(qh * r * gq_ref[...], rc_ref[...], rs_ref[...]).astype(_BF)

        def k_rope():
            kr = raw_kr_ref[...]
            shared["ss_r"] = jnp.sum(jnp.where(qk_lanes, kr * kr, 0.0), axis=-1, keepdims=True)
            shared["krr"] = _rope(kr * gkr_ref[...], rc_ref[...], rs_ref[...])

        def k_head(h):
            blk = slice(LANES * h, LANES * (h + 1))
            kh = raw_k_ref[:, blk]
            ss = jnp.sum(kh * kh, axis=-1, keepdims=True) + shared["ss_r"]
            r = lax.rsqrt(ss * (1.0 / MLA_QK) + EPS)
            km_ref[:, blk] = ((kh * gkn_ref[...] + shared["krr"]) * r).astype(_BF)

        def na_block(raw_ref, g_ref, out_ref, j):
            blk = slice(LANES * j, LANES * (j + 1))
            out_ref[:, blk] = _half_head_norm(raw_ref[:, blk], g_ref[...], lo)

        pieces = [lambda h=h: q_head(h) for h in range(HEADS)] + [k_rope]
        pieces += [lambda h=h: k_head(h) for h in range(HEADS)]
        for raw_ref, g_ref, out_ref in ((raw_qn_ref, gqna_ref, qn_ref), (raw_kn_ref, gkna_ref, kn_ref)):
            pieces += [lambda a=(raw_ref, g_ref, out_ref, j): na_block(*a) for j in range(HALF_WIDTH // LANES)]
        return pieces

    @pl.when(step == 0)
    def _():
        for ref in raw_refs:
            ref[...] = jnp.zeros_like(ref)

    @pl.when(step < last)
    def _():
        xn = _rms(x_ref[...], g1_ref[...]).astype(_BF)
        _swiglu_hidden(xn, wg_ref, wu_ref, act_ref, side_work=head_work())

    @pl.when(step + 1 <= last)
    def _():
        h1 = x_ref[...] + 0.5 * _swiglu_down(act_ref, wd_ref)
        h1_ref[...] = h1
        xn = _rms(h1, gmix_ref[...]).astype(_BF)
        cq = _dot(xn, win_ref[:, _O_CQ:_O_CQ + Q_LORA])
        raw_q_ref[...] = _dot(_rms(cq, qag_ref[...]).astype(_BF), wuq_ref[...])
        ckv = _dot(xn, win_ref[:, _O_CKV:_O_CKV + KV_LORA])
        kvu = _dot(_rms(ckv, kvag_ref[...]).astype(_BF), wukv_ref[...])
        raw_k_ref[...] = kvu[:, :HEADS * LANES]
        vm_ref[...] = kvu[:, HEADS * LANES:].astype(_BF)
        raw_kr_ref[...] = _dot(xn, win_ref[:, _O_KR:_O_KR + LANES])
        raw_qn_ref[...] = _dot(xn, win_ref[:, _O_QNA:_O_QNA + HALF_WIDTH])
        raw_kn_ref[...] = _dot(xn, win_ref[:, _O_KNA:_O_KNA + HALF_WIDTH])
        vn_ref[...] = _dot(xn, win_ref[:, _O_VNA:_O_VNA + HALF_WIDTH]).astype(_BF)

    @pl.when(step == last)
    def _():
        for piece in head_work():
            piece()


def _const_spec(shape):
    nd = len(shape)
    return pl.BlockSpec(shape, lambda i: (0,) * nd, pipeline_mode=pl.Buffered(1))


def _ffn1_proj(x2d, rope_tabs, p, tile, vmem_mib):
    n_tiles = x2d.shape[0] // tile
    tabs_per_seq = rope_tabs[0].shape[0] // tile
    cur = lambda i: jnp.minimum(i, n_tiles - 1)
    prev = lambda i: jnp.maximum(i - 1, 0)
    cur_spec = lambda w: pl.BlockSpec((tile, w), lambda i: (cur(i), 0))
    prev_spec = lambda w: pl.BlockSpec((tile, w), lambda i: (prev(i), 0))
    tab_spec = pl.BlockSpec((tile, LANES), lambda i: (prev(i) % tabs_per_seq, 0))
    consts = [p["g1"], p["wg1"], p["wu1"], p["wd1"], p["gmix"], p["win_r"], p["qag"], p["wuq"],
              p["kvag"], p["wukv"], p["gq"], p["gkn"], p["gkr"], p["gqna"], p["gkna"]]
    wide = HEADS * LANES
    widths = [wide, wide, HALF_WIDTH, HALF_WIDTH, HALF_WIDTH, HALF_WIDTH]
    out_specs = [cur_spec(D_MODEL), prev_spec(wide), prev_spec(wide), cur_spec(HALF_WIDTH),
                 prev_spec(HALF_WIDTH), prev_spec(HALF_WIDTH), cur_spec(HALF_WIDTH)]
    out_shape = [jax.ShapeDtypeStruct((x2d.shape[0], D_MODEL), _F32)]
    out_shape += [jax.ShapeDtypeStruct((x2d.shape[0], w), _BF) for w in widths]
    raw = lambda w: pltpu.VMEM((tile, w), _F32)
    return pl.pallas_call(
        _ffn1_proj_kernel,
        grid=(n_tiles + 1,),
        in_specs=[cur_spec(D_MODEL)] + [_const_spec(c.shape) for c in consts] + [tab_spec] * 2,
        out_specs=out_specs,
        out_shape=out_shape,
        scratch_shapes=[pltpu.VMEM((N_FF_CHUNKS, tile, FF_CHUNK), _BF),
                        raw(wide), raw(wide), raw(LANES), raw(HALF_WIDTH), raw(HALF_WIDTH)],
        compiler_params=pltpu.CompilerParams(dimension_semantics=("arbitrary",),
                                             vmem_limit_bytes=vmem_mib * MIB),
        name="ffn1_proj",
    )(x2d, *consts, *rope_tabs)


def _extend_pair(v_pair, dst_ref):
    dst_ref[:, :LANES] = v_pair
    dst_ref[:, LANES:] = jnp.ones_like(v_pair)


def _mla_kernel(q_ref, k_ref, v_ref, km_ref, vm_ref, o_ref, vext_ref, vmext_ref):
    @pl.when(pl.program_id(1) == 0)
    def _():
        for hp in range(HEADS // 2):
            _extend_pair(v_ref[0, :, LANES * hp:LANES * (hp + 1)], vext_ref.at[hp])
            _extend_pair(vm_ref[:, LANES * hp:LANES * (hp + 1)], vmext_ref.at[hp])

    n_tiles = SEQ // MLA_K_TILE
    n_groups = n_tiles // MLA_GROUP_TILES
    slots = HEADS * n_tiles
    scores, meta_scores, group_max, parts, outs = {}, {}, {}, {}, {}
    run_max = None
    for g in range(slots + MLA_PV_LAG):
        if g < slots:
            h, t = divmod(g, n_tiles)
            sl = slice(LANES * h, LANES * (h + 1))
            if t == 0:
                q = q_ref[0, :, sl]
                meta_scores[h] = _dot_nt(q, km_ref[:, sl])
            s = _dot_nt(q, k_ref[0, MLA_K_TILE * t:MLA_K_TILE * (t + 1), sl])
            run_max = s if t % MLA_GROUP_TILES == 0 else jnp.maximum(run_max, s)
            scores[(h, t)] = s
            if t % MLA_GROUP_TILES == MLA_GROUP_TILES - 1:
                m = jnp.max(run_max, axis=-1, keepdims=True)
                if t // MLA_GROUP_TILES == 0:
                    m = jnp.maximum(m, jnp.max(meta_scores[h], axis=-1, keepdims=True))
                group_max[(h, t // MLA_GROUP_TILES)] = m
        gp = g - MLA_PV_LAG
        if gp >= 0:
            h, t = divmod(gp, n_tiles)
            j = t // MLA_GROUP_TILES
            m = group_max[(h, j)]
            part = _dot(jnp.exp2(scores.pop((h, t)) - m).astype(_BF),
                        vext_ref[h // 2, MLA_K_TILE * t:MLA_K_TILE * (t + 1), :])
            if t == 0:
                part += _dot(jnp.exp2(meta_scores[h] - m).astype(_BF), vmext_ref[h // 2])
            parts[(h, j)] = part if t % MLA_GROUP_TILES == 0 else parts[(h, j)] + part
            if t == n_tiles - 1:
                m_all = group_max[(h, 0)]
                for jj in range(1, n_groups):
                    m_all = jnp.maximum(m_all, group_max[(h, jj)])
                acc = sum(jnp.exp2(group_max[(h, jj)] - m_all) * parts.pop((h, jj)) for jj in range(n_groups))
                outs[h] = acc[:, :LANES] / acc[:, LANES:LANES + 1]
    lo = _lane_iota(MLA_Q_TILE) < HALF_LANES
    for hp in range(HEADS // 2):
        o_ref[0, :, LANES * hp:LANES * (hp + 1)] = jnp.where(lo, outs[2 * hp], outs[2 * hp + 1]).astype(_BF)


def _mla_attn(q, k, v, k_meta, v_meta):
    b = q.shape[0]
    once = pl.Buffered(1)
    whole = lambda shape: pl.BlockSpec(shape, lambda bi, i: (0,) * len(shape), pipeline_mode=once)
    return pl.pallas_call(
        _mla_kernel,
        grid=(b, SEQ // MLA_Q_TILE),
        in_specs=[pl.BlockSpec((1, MLA_Q_TILE, HEADS * LANES), lambda bi, i: (bi, i, 0)),
                  pl.BlockSpec((1, SEQ, HEADS * LANES), lambda bi, i: (bi, 0, 0), pipeline_mode=once),
                  pl.BlockSpec((1, SEQ, HALF_WIDTH), lambda bi, i: (bi, 0, 0), pipeline_mode=once),
                  whole((N_META, HEADS * LANES)), whole((N_META, HALF_WIDTH))],
        out_specs=pl.BlockSpec((1, MLA_Q_TILE, HALF_WIDTH), lambda bi, i: (bi, i, 0)),
        out_shape=jax.ShapeDtypeStruct((b, SEQ, HALF_WIDTH), _BF),
        scratch_shapes=[pltpu.VMEM((HEADS // 2, SEQ, 2 * LANES), _BF),
                        pltpu.VMEM((HEADS // 2, N_META, 2 * LANES), _BF)],
        compiler_params=pltpu.CompilerParams(dimension_semantics=("arbitrary",) * 2,
                                             vmem_limit_bytes=VMEM_MIB["mla_attn"] * MIB),
        name="mla_attn",
    )(q, k, v, k_meta, v_meta)


def _na_slab_row(g):
    return int(np.clip(g * NA_GROUP_ROWS - NA_MAX_ROWS // 2, 0, NA_ROWS - NA_WIN_ROWS))


def _na_kernel(q_ref, k_ref, v_ref, km_ref, vm_ref, tab_ref, o_ref, ks_ref, vs_ref):
    @pl.when(pl.program_id(1) == 0)
    def _():
        for buf in range(NA_GROUPS):
            ks_ref[buf, NA_WIN:, :] = jnp.zeros((NA_KEYS - NA_WIN, LANES), _BF)
            vs_ref[buf, NA_WIN:, :] = jnp.zeros((NA_KEYS - NA_WIN, 2 * LANES), _BF)
            ks_ref[buf, NA_WIN:NA_WIN + N_META, :] = km_ref[...]
            vs_ref[buf, NA_WIN:NA_WIN + N_META, :LANES] = vm_ref[...]
            vs_ref[buf, :NA_WIN + N_META, LANES:] = jnp.ones((NA_WIN + N_META, LANES), _BF)

    lo = _lane_iota(NA_GROUP) < HALF_LANES

    scores, outs = {}, {}
    items = 2 * NA_GROUPS
    for n in range(items + 1):
        if n < items:
            g, hh = divmod(n, 2)
            if hh == 0:
                rows = slice(_na_slab_row(g) * GRID_W, _na_slab_row(g) * GRID_W + NA_WIN)
                ks_ref[g, :NA_WIN, :] = k_ref[0, rows, :]
                vs_ref[g, :NA_WIN, :LANES] = v_ref[0, rows, :]
                q_pair = q_ref[0, NA_GROUP * g:NA_GROUP * (g + 1), :]
            q = jnp.where(lo == (hh == 0), q_pair, jnp.zeros_like(q_pair))
            variant = 0 if g == 0 else (2 if g == NA_GROUPS - 1 else 1)
            scores[n] = _dot_nt(q, ks_ref[g]) + tab_ref[hh, variant]
        if n >= 1:
            g, hh = divmod(n - 1, 2)
            s = scores.pop(n - 1)
            m = jnp.max(s, axis=-1, keepdims=True)
            acc = _dot(jnp.exp2(s - m).astype(_BF), vs_ref[g])
            outs[hh] = acc[:, :LANES] / acc[:, LANES:LANES + 1]
            if hh == 1:
                o_ref[0, NA_GROUP * g:NA_GROUP * (g + 1), :] = jnp.where(lo, outs[0], outs[1]).astype(_BF)


def _na_attn(q, k, v, k_meta, v_meta, table):
    b = q.shape[0]
    seq_spec = pl.BlockSpec((1, SEQ, LANES), lambda hp, bi: (bi, 0, hp))
    meta_spec = pl.BlockSpec((N_META, LANES), lambda hp, bi: (0, hp))
    return pl.pallas_call(
        _na_kernel,
        grid=(HEADS // 2, b),
        in_specs=[seq_spec, seq_spec, seq_spec, meta_spec, meta_spec,
                  pl.BlockSpec((2, 3, NA_GROUP, NA_KEYS), lambda hp, bi: (hp, 0, 0, 0))],
        out_specs=seq_spec,
        out_shape=jax.ShapeDtypeStruct((b, SEQ, HALF_WIDTH), _BF),
        scratch_shapes=[pltpu.VMEM((NA_GROUPS, NA_KEYS, LANES), _BF),
                        pltpu.VMEM((NA_GROUPS, NA_KEYS, 2 * LANES), _BF)],
        compiler_params=pltpu.CompilerParams(dimension_semantics=("arbitrary",) * 2,
                                             vmem_limit_bytes=VMEM_MIB["na_attn"] * MIB),
        name="na_attn",
    )(q, k, v, k_meta, v_meta, table)


def _na_bias_table(rel_bias):
    n_dr, n_dc = 2 * NA_MAX_ROWS - 1, 2 * NA_KW - 1
    c = np.arange(GRID_W)
    cs = np.clip(c - NA_KW // 2, 0, GRID_W - NA_KW)
    col_ok = (c[None, :] >= cs[:, None]) & (c[None, :] < cs[:, None] + NA_KW)
    dc = c[None, :] - c[:, None] + NA_KW - 1
    col_pick = (dc[None] == np.arange(n_dc)[:, None, None]) & col_ok[None]
    row_pick = np.full((3, NA_GROUP_ROWS, NA_WIN_ROWS), -1)
    for v, g in enumerate((0, 1, NA_GROUPS - 1)):
        for i in range(NA_GROUP_ROWS):
            r = g * NA_GROUP_ROWS + i
            rs = int(np.clip(r - NA_MAX_ROWS // 2, 0, NA_ROWS - NA_MAX_ROWS))
            for j in range(NA_WIN_ROWS):
                kr = _na_slab_row(g) + j
                if rs <= kr < rs + NA_MAX_ROWS:
                    row_pick[v, i, j] = kr - r + NA_MAX_ROWS - 1
    row_pick = row_pick.reshape(-1)
    row_onehot = (row_pick[:, None] == np.arange(n_dr)[None, :]).astype(np.float32)
    exact = lax.Precision.HIGHEST
    lines = jnp.einsum("pr,hrd->hpd", jnp.asarray(row_onehot), rel_bias * LOG2E, precision=exact)
    blocks = jnp.einsum("hpd,dcx->hpcx", lines, jnp.asarray(col_pick.astype(np.float32)), precision=exact)
    inside = (row_pick >= 0)[:, None, None] & col_ok[None]
    blocks = jnp.where(inside[None], blocks, MASKED)
    win = blocks.reshape(HEADS, 3, NA_GROUP_ROWS, NA_WIN_ROWS, GRID_W, GRID_W)
    win = win.transpose(0, 1, 2, 4, 3, 5).reshape(HEADS, 3, NA_GROUP, NA_WIN)
    tail = jnp.concatenate([jnp.zeros((N_META,), _F32), jnp.full((NA_KEYS - NA_WIN - N_META,), MASKED, _F32)])
    return jnp.concatenate([win, jnp.broadcast_to(tail, (HEADS, 3, NA_GROUP, tail.shape[0]))], axis=-1)


def _merge_ffn2_kernel(h1_ref, oa_ref, ob_ref, gmix_ref, wga_ref, wgb_ref, wba_ref, wbb_ref, wout_ref,
                       g2_ref, wg_ref, wu_ref, wd_ref, out_ref, act_ref):
    h1 = h1_ref[...]
    xn = _rms(h1, gmix_ref[...]).astype(_BF)
    merged = (_sigmoid(_dot(xn, wga_ref[...])) * _dot(oa_ref[...], wba_ref[...])
              + _sigmoid(_dot(xn, wgb_ref[...])) * _dot(ob_ref[...], wbb_ref[...]))
    h2 = h1 + _dot(merged.astype(_BF), wout_ref[...])
    _swiglu_hidden(_rms(h2, g2_ref[...]).astype(_BF), wg_ref, wu_ref, act_ref)
    out_ref[...] = h2 + 0.5 * _swiglu_down(act_ref, wd_ref)


def _merge_ffn2(h1, oa, ob, p):
    n = h1.shape[0]
    row_spec = lambda w: pl.BlockSpec((TOKEN_TILE, w), lambda i: (i, 0))
    consts = [p["gmix"], p["wga"], p["wgb"], p["wba"], p["wbb"], p["wout"], p["g2"], p["wg2"], p["wu2"], p["wd2"]]
    return pl.pallas_call(
        _merge_ffn2_kernel,
        grid=(n // TOKEN_TILE,),
        in_specs=[row_spec(D_MODEL), row_spec(HALF_WIDTH), row_spec(HALF_WIDTH)]
                 + [_const_spec(c.shape) for c in consts],
        out_specs=row_spec(D_MODEL),
        out_shape=jax.ShapeDtypeStruct((n, D_MODEL), _F32),
        scratch_shapes=[pltpu.VMEM((N_FF_CHUNKS, TOKEN_TILE, FF_CHUNK), _BF)],
        compiler_params=pltpu.CompilerParams(dimension_semantics=("arbitrary",),
                                             vmem_limit_bytes=VMEM_MIB["merge_ffn2"] * MIB),
        name="merge_ffn2",
    )(h1, oa, ob, *consts)


def _rope_swapped(w):
    half = MLA_ROPE // 2
    return jnp.concatenate([w, w[..., half:], w[..., :half]], axis=-1)


def _lane_row(v, offset):
    return jnp.zeros((1, LANES), _F32).at[0, offset:offset + v.shape[0]].set(v)


def _ff_weights(w_gate, w_up, w_down):
    return w_gate.astype(_BF), w_up.astype(_BF), w_down.reshape(N_FF_CHUNKS, FF_CHUNK, D_MODEL).astype(_BF)


def _rope_tables():
    half = MLA_ROPE // 2
    pos = jnp.arange(N_META + SEQ, dtype=_F32)
    inv = ROPE_THETA ** (-jnp.arange(half, dtype=_F32) / half)
    ang = pos[:, None] * inv[None, :]
    cos, sin = jnp.cos(ang), jnp.sin(ang)
    n = pos.shape[0]
    tail = jnp.zeros((n, LANES - MLA_QK), _F32)
    c = jnp.concatenate([jnp.ones((n, MLA_NOPE), _F32), cos, cos, tail], axis=1)
    s = jnp.concatenate([jnp.zeros((n, MLA_NOPE), _F32), -sin, sin, tail], axis=1)
    return c, s


def kernel(x, meta_tokens, ffn1_norm, ffn1_w_gate, ffn1_w_up, ffn1_w_down, mix_norm, w_in, mla_q_a_norm, mla_w_uq,
           mla_kv_a_norm, mla_w_ukv, mla_q_head_norm, mla_k_head_norm, na_q_head_norm, na_k_head_norm, na_rel_bias,
           w_branch_a, w_branch_b, w_out, ffn2_norm, ffn2_w_gate, ffn2_w_up, ffn2_w_down):
    assert ffn1_norm.shape[0] == 1, "single-layer block"
    b = x.shape[0]
    row = lambda v: v.reshape(1, -1).astype(_F32)
    splits = np.cumsum((Q_LORA, KV_LORA, MLA_ROPE, HALF_WIDTH, HALF_WIDTH, HALF_WIDTH, D_MODEL, D_MODEL))
    w_cq, w_ckv, w_kr, w_qna, w_kna, w_vna, w_ga, w_gb = jnp.split(w_in[0], splits[:-1].tolist(), axis=1)
    w_kr_blk = jnp.pad(_rope_swapped(w_kr), ((0, 0), (MLA_NOPE, 0)))
    uq = mla_w_uq[0].reshape(Q_LORA, HEADS, MLA_QK)
    uq = jnp.concatenate([uq[..., :MLA_NOPE], _rope_swapped(uq[..., MLA_NOPE:])], axis=-1).reshape(Q_LORA, -1)
    kv_up = mla_w_ukv[0].reshape(KV_LORA, HEADS, 2 * MLA_NOPE)
    k_nope_up = jnp.pad(kv_up[..., :MLA_NOPE], ((0, 0), (0, 0), (0, LANES - MLA_NOPE))).reshape(KV_LORA, -1)
    gq_mla = mla_q_head_norm[0] * (MLA_QK ** -0.5 * LOG2E)
    gk_mla = mla_k_head_norm[0]
    gq_na = na_q_head_norm[0] * (NA_HEAD_DIM ** -0.5 * LOG2E)
    p = {
        "g1": row(ffn1_norm[0]), "gmix": row(mix_norm[0]), "g2": row(ffn2_norm[0]),
        "win_r": jnp.concatenate([w_cq, w_ckv, w_kr_blk, w_qna, w_kna, w_vna], axis=1).astype(_BF),
        "qag": row(mla_q_a_norm[0]), "kvag": row(mla_kv_a_norm[0]),
        "wuq": uq.astype(_BF),
        "wukv": jnp.concatenate([k_nope_up, kv_up[..., MLA_NOPE:].reshape(KV_LORA, -1)], axis=1).astype(_BF),
        "gq": row(jnp.concatenate([gq_mla[:MLA_NOPE], _rope_swapped(gq_mla[MLA_NOPE:])])),
        "gkn": _lane_row(gk_mla[:MLA_NOPE], 0),
        "gkr": _lane_row(_rope_swapped(gk_mla[MLA_NOPE:]), MLA_NOPE),
        "gqna": row(jnp.concatenate([gq_na, gq_na])),
        "gkna": row(jnp.concatenate([na_k_head_norm[0], na_k_head_norm[0]])),
        "wga": w_ga.astype(_BF), "wgb": w_gb.astype(_BF),
        "wba": w_branch_a[0].astype(_BF), "wbb": w_branch_b[0].astype(_BF), "wout": w_out[0].astype(_BF),
    }
    p["wg1"], p["wu1"], p["wd1"] = _ff_weights(ffn1_w_gate[0], ffn1_w_up[0], ffn1_w_down[0])
    p["wg2"], p["wu2"], p["wd2"] = _ff_weights(ffn2_w_gate[0], ffn2_w_up[0], ffn2_w_down[0])
    assert p["win_r"].shape[1] == _IN_R_WIDTH and p["wuq"].shape[1] == HEADS * LANES

    tabs = _rope_tables()
    meta = _ffn1_proj(meta_tokens.astype(_F32), [t[:N_META] for t in tabs], p, N_META, VMEM_MIB["ffn1_proj_meta"])
    _, _, km_meta, vm_meta, _, kn_meta, vn_meta = meta
    real = _ffn1_proj(x.reshape(b * SEQ, D_MODEL), [t[N_META:] for t in tabs], p, TOKEN_TILE, VMEM_MIB["ffn1_proj"])
    h1, qm, km, vm, qn, kn, vn = real
    seq = lambda a: a.reshape(b, SEQ, a.shape[-1])
    o_a = _mla_attn(seq(qm), seq(km), seq(vm), km_meta, vm_meta)
    o_b = _na_attn(seq(qn), seq(kn), seq(vn), kn_meta, vn_meta, _na_bias_table(na_rel_bias[0]))
    out = _merge_ffn2(h1, o_a.reshape(b * SEQ, HALF_WIDTH), o_b.reshape(b * SEQ, HALF_WIDTH), p)
    return out.reshape(b, SEQ, D_MODEL)
```

```python
import math

import jax
import jax.numpy as jnp
import numpy as np
from jax import lax
from jax.experimental import pallas as pl
from jax.experimental.pallas import tpu as pltpu

D_MODEL = 1024
SEQ = 4096
N_META = 16
GRID_W = 64
HEADS = 8
MLA_NOPE = 64
MLA_ROPE = 32
MLA_QK = MLA_NOPE + MLA_ROPE
Q_LORA = 384
KV_LORA = 256
NA_HEAD_DIM = 64
NA_MAX_ROWS = 8
NA_KW = 16
D_FF = 2816
ROPE_THETA = 10000.0
EPS = 1e-6
HALF_WIDTH = 512
LOG2E = math.log2(math.e)

LANES = 128
HALF_LANES = LANES // 2
FF_CHUNK = 256
N_FF_CHUNKS = D_FF // FF_CHUNK
assert N_FF_CHUNKS * FF_CHUNK == D_FF

TOKEN_TILE = 512
MLA_Q_TILE = 256
MLA_K_TILE = 256
MLA_GROUP_TILES = 8
MLA_PV_LAG = 12
MLA_P_SHIFT = 7.0
MLA_V_TARGET = 256.0
NA_GROUP_ROWS = 4
NA_GROUP = NA_GROUP_ROWS * GRID_W
NA_WIN_ROWS = NA_GROUP_ROWS + NA_MAX_ROWS - 1
NA_WIN = NA_WIN_ROWS * GRID_W
NA_KEYS = 768
assert NA_WIN + N_META <= NA_KEYS and NA_KEYS % 256 == 0
NA_ROWS = SEQ // GRID_W
NA_GROUPS = NA_ROWS // NA_GROUP_ROWS
MASKED = -1e30

MIB = 2 ** 20
V7X_VMEM_MIB = 64
VMEM_MIB = {"ffn1_proj": 56, "ffn1_proj_meta": 40, "mla_attn": 56, "na_attn": 48, "merge_ffn2": 56}
assert max(VMEM_MIB.values()) < V7X_VMEM_MIB

_O_CQ, _O_CKV, _O_KR = 0, Q_LORA, Q_LORA + KV_LORA
_O_QNA = _O_KR + LANES
_O_KNA = _O_QNA + HALF_WIDTH
_O_VNA = _O_KNA + HALF_WIDTH
_IN_R_WIDTH = _O_VNA + HALF_WIDTH

_BF = jnp.bfloat16
_F32 = jnp.float32
_F8 = jnp.float8_e4m3fn


def _dot(a, b):
    return jnp.dot(a, b, preferred_element_type=_F32)


def _dot_nt(a, b):
    return lax.dot_general(a, b, (((1,), (1,)), ((), ())), preferred_element_type=_F32)


def _sigmoid(x):
    return 1.0 / (1.0 + jnp.exp(-x))


def _rms(x, g):
    return x * lax.rsqrt(jnp.mean(x * x, axis=-1, keepdims=True) + EPS) * g


def _lane_iota(rows):
    return lax.broadcasted_iota(jnp.int32, (rows, LANES), 1)


def _swiglu_hidden(xn, wg_ref, wu_ref, act_ref, side_work=()):
    side_work = list(side_work)
    per_chunk = -(-len(side_work) // N_FF_CHUNKS)
    for c in range(N_FF_CHUNKS):
        cols = slice(FF_CHUNK * c, FF_CHUNK * (c + 1))
        g = _dot(xn, wg_ref[:, cols])
        u = _dot(xn, wu_ref[:, cols])
        act_ref[c] = (g * _sigmoid(g) * u).astype(_BF)
        for piece in side_work[c * per_chunk:(c + 1) * per_chunk]:
            piece()


def _swiglu_down(act_ref, wd_ref):
    y = _dot(act_ref[0], wd_ref[0])
    for c in range(1, N_FF_CHUNKS):
        y += _dot(act_ref[c], wd_ref[c])
    return y


def _rope(x, c, s):
    return x * c + pltpu.roll(x, LANES - MLA_ROPE, 1) * s


def _half_head_norm(blk, g, lo):
    sq = blk * blk
    s_lo = jnp.sum(jnp.where(lo, sq, 0.0), axis=-1, keepdims=True)
    s_hi = jnp.sum(jnp.where(lo, 0.0, sq), axis=-1, keepdims=True)
    r = lax.rsqrt(jnp.where(lo, s_lo, s_hi) * (1.0 / NA_HEAD_DIM) + EPS)
    return (blk * r * g).astype(_BF)


def _ffn1_proj_kernel(x_ref, g1_ref, wg_ref, wu_ref, wd_ref, gmix_ref, win_ref, qag_ref, wuq_ref,
                      kvag_ref, wukv_ref, gq_ref, gkn_ref, gkr_ref, gqna_ref, gkna_ref, rc_ref, rs_ref,
                      h1_ref, qm_ref, km_ref, vm_ref, qn_ref, kn_ref, vn_ref,
                      act_ref, raw_q_ref, raw_k_ref, raw_kr_ref, raw_qn_ref, raw_kn_ref):
    step = pl.program_id(0)
    last = pl.num_programs(0) - 1
    rows = x_ref.shape[0]
    raw_refs = (raw_q_ref, raw_k_ref, raw_kr_ref, raw_qn_ref, raw_kn_ref)

    def head_work():
        qk_lanes = _lane_iota(rows) < MLA_QK
        lo = _lane_iota(rows) < HALF_LANES
        shared = {}

        def q_head(h):
            blk = slice(LANES * h, LANES * (h + 1))
            qh = raw_q_ref[:, blk]
            ss = jnp.sum(jnp.where(qk_lanes, qh * qh, 0.0), axis=-1, keepdims=True)
            r = lax.rsqrt(ss * (1.0 / MLA_QK) + EPS)
            qm_ref[:, blk] = _rope(qh * r * gq_ref[...], rc_ref[...], rs_ref[...]).astype(_BF)

        def k_rope():
            kr = raw_kr_ref[...]
            shared["ss_r"] = jnp.sum(jnp.where(qk_lanes, kr * kr, 0.0), axis=-1, keepdims=True)
            shared["krr"] = _rope(kr * gkr_ref[...], rc_ref[...], rs_ref[...])

        def k_head(h):
            blk = slice(LANES * h, LANES * (h + 1))
            kh = raw_k_ref[:, blk]
            ss = jnp.sum(kh * kh, axis=-1, keepdims=True) + shared["ss_r"]
            r = lax.rsqrt(ss * (1.0 / MLA_QK) + EPS)
            km_ref[:, blk] = ((kh * gkn_ref[...] + shared["krr"]) * r).astype(_BF)

        def na_block(raw_ref, g_ref, out_ref, j):
            blk = slice(LANES * j, LANES * (j + 1))
            out_ref[:, blk] = _half_head_norm(raw_ref[:, blk], g_ref[...], lo)

        pieces = [lambda h=h: q_head(h) for h in range(HEADS)] + [k_rope]
        pieces += [lambda h=h: k_head(h) for h in range(HEADS)]
        for raw_ref, g_ref, out_ref in ((raw_qn_ref, gqna_ref, qn_ref), (raw_kn_ref, gkna_ref, kn_ref)):
            pieces += [lambda a=(raw_ref, g_ref, out_ref, j): na_block(*a) for j in range(HALF_WIDTH // LANES)]
        return pieces

    @pl.when(step == 0)
    def _():
        for ref in raw_refs:
            ref[...] = jnp.zeros_like(ref)

    @pl.when(step < last)
    def _():
        xn = _rms(x_ref[...], g1_ref[...]).astype(_BF)
        _swiglu_hidden(xn, wg_ref, wu_ref, act_ref, side_work=head_work())

    @pl.when(step + 1 <= last)
    def _():
        h1 = x_ref[...] + 0.5 * _swiglu_down(act_ref, wd_ref)
        h1_ref[...] = h1
        xn = _rms(h1, gmix_ref[...]).astype(_BF)
        cq = _dot(xn, win_ref[:, _O_CQ:_O_CQ + Q_LORA])
        raw_q_ref[...] = _dot(_rms(cq, qag_ref[...]).astype(_BF), wuq_ref[...])
        ckv = _dot(xn, win_ref[:, _O_CKV:_O_CKV + KV_LORA])
        kvu = _dot(_rms(ckv, kvag_ref[...]).astype(_BF), wukv_ref[...])
        raw_k_ref[...] = kvu[:, :HEADS * LANES]
        vm_ref[...] = kvu[:, HEADS * LANES:].astype(_BF)
        raw_kr_ref[...] = _dot(xn, win_ref[:, _O_KR:_O_KR + LANES])
        raw_qn_ref[...] = _dot(xn, win_ref[:, _O_QNA:_O_QNA + HALF_WIDTH])
        raw_kn_ref[...] = _dot(xn, win_ref[:, _O_KNA:_O_KNA + HALF_WIDTH])
        vn_ref[...] = _dot(xn, win_ref[:, _O_VNA:_O_VNA + HALF_WIDTH]).astype(_BF)

    @pl.when(step == last)
    def _():
        for piece in head_work():
            piece()


def _const_spec(shape):
    nd = len(shape)
    return pl.BlockSpec(shape, lambda i: (0,) * nd, pipeline_mode=pl.Buffered(1))


def _ffn1_proj(x2d, rope_tabs, p, tile, vmem_mib):
    n_tiles = x2d.shape[0] // tile
    tabs_per_seq = rope_tabs[0].shape[0] // tile
    cur = lambda i: jnp.minimum(i, n_tiles - 1)
    prev = lambda i: jnp.maximum(i - 1, 0)
    cur_spec = lambda w: pl.BlockSpec((tile, w), lambda i: (cur(i), 0))
    prev_spec = lambda w: pl.BlockSpec((tile, w), lambda i: (prev(i), 0))
    tab_spec = pl.BlockSpec((tile, LANES), lambda i: (prev(i) % tabs_per_seq, 0))
    consts = [p["g1"], p["wg1"], p["wu1"], p["wd1"], p["gmix"], p["win_r"], p["qag"], p["wuq"],
              p["kvag"], p["wukv"], p["gq"], p["gkn"], p["gkr"], p["gqna"], p["gkna"]]
    wide = HEADS * LANES
    widths = [wide, wide, HALF_WIDTH, HALF_WIDTH, HALF_WIDTH, HALF_WIDTH]
    out_specs = [cur_spec(D_MODEL), prev_spec(wide), prev_spec(wide), cur_spec(HALF_WIDTH),
                 prev_spec(HALF_WIDTH), prev_spec(HALF_WIDTH), cur_spec(HALF_WIDTH)]
    out_shape = [jax.ShapeDtypeStruct((x2d.shape[0], D_MODEL), _F32)]
    out_shape += [jax.ShapeDtypeStruct((x2d.shape[0], w), _BF) for w in widths]
    raw = lambda w: pltpu.VMEM((tile, w), _F32)
    return pl.pallas_call(
        _ffn1_proj_kernel,
        grid=(n_tiles + 1,),
        in_specs=[cur_spec(D_MODEL)] + [_const_spec(c.shape) for c in consts] + [tab_spec] * 2,
        out_specs=out_specs,
        out_shape=out_shape,
        scratch_shapes=[pltpu.VMEM((N_FF_CHUNKS, tile, FF_CHUNK), _BF),
                        raw(wide), raw(wide), raw(LANES), raw(HALF_WIDTH), raw(HALF_WIDTH)],
        compiler_params=pltpu.CompilerParams(dimension_semantics=("arbitrary",),
                                             vmem_limit_bytes=vmem_mib * MIB),
        name="ffn1_proj",
    )(x2d, *consts, *rope_tabs)


def _extend_pair(v_pair, dst_ref):
    dst_ref[:, :LANES] = v_pair
    dst_ref[:, LANES:] = jnp.ones_like(v_pair)


def _mla_kernel(q_ref, k_ref, v_ref, km_ref, vm_ref, o_ref, vext_ref, vmext_ref, vscale_ref):
    @pl.when(pl.program_id(1) == 0)
    def _():
        lo_row = lax.broadcasted_iota(jnp.int32, (1, LANES), 1) < HALF_LANES
        for hp in range(HEADS // 2):
            blk = slice(LANES * hp, LANES * (hp + 1))
            v_pair, vm_pair = v_ref[0, :, blk].astype(_F32), vm_ref[:, blk].astype(_F32)
            amax = jnp.maximum(jnp.max(jnp.abs(v_pair), axis=0, keepdims=True),
                               jnp.max(jnp.abs(vm_pair), axis=0, keepdims=True))
            head_max = jnp.where(lo_row, jnp.max(jnp.where(lo_row, amax, 0.0), axis=1, keepdims=True),
                                 jnp.max(jnp.where(lo_row, 0.0, amax), axis=1, keepdims=True))
            scale = jnp.maximum(head_max, 1e-30) * (1.0 / MLA_V_TARGET)
            vscale_ref[hp] = scale
            vext_ref[hp, :, :LANES] = (v_pair / scale).astype(_F8)
            vext_ref[hp, :, LANES:] = jnp.ones((SEQ, LANES), _F8)
            vmext_ref[hp, :, :LANES] = (vm_pair / scale).astype(_BF)
            vmext_ref[hp, :, LANES:] = jnp.ones((N_META, LANES), _BF)

    n_tiles = SEQ // MLA_K_TILE
    n_groups = n_tiles // MLA_GROUP_TILES
    slots = HEADS * n_tiles
    scores, meta_scores, group_max, parts, outs = {}, {}, {}, {}, {}
    run_max = None
    for g in range(slots + MLA_PV_LAG):
        if g < slots:
            h, t = divmod(g, n_tiles)
            sl = slice(LANES * h, LANES * (h + 1))
            if t == 0:
                q = q_ref[0, :, sl]
                meta_scores[h] = _dot_nt(q, km_ref[:, sl])
            s = _dot_nt(q, k_ref[0, MLA_K_TILE * t:MLA_K_TILE * (t + 1), sl])
            run_max = s if t % MLA_GROUP_TILES == 0 else jnp.maximum(run_max, s)
            scores[(h, t)] = s
            if t % MLA_GROUP_TILES == MLA_GROUP_TILES - 1:
                m = jnp.max(run_max, axis=-1, keepdims=True)
                if t // MLA_GROUP_TILES == 0:
                    m = jnp.maximum(m, jnp.max(meta_scores[h], axis=-1, keepdims=True))
                group_max[(h, t // MLA_GROUP_TILES)] = m
        gp = g - MLA_PV_LAG
        if gp >= 0:
            h, t = divmod(gp, n_tiles)
            j = t // MLA_GROUP_TILES
            m = group_max[(h, j)]
            part = _dot(jnp.exp2(scores.pop((h, t)) - m + MLA_P_SHIFT).astype(_F8),
                        vext_ref[h // 2, MLA_K_TILE * t:MLA_K_TILE * (t + 1), :])
            if t == 0:
                part += _dot(jnp.exp2(meta_scores[h] - m + MLA_P_SHIFT).astype(_BF), vmext_ref[h // 2])
            parts[(h, j)] = part if t % MLA_GROUP_TILES == 0 else parts[(h, j)] + part
            if t == n_tiles - 1:
                m_all = group_max[(h, 0)]
                for jj in range(1, n_groups):
                    m_all = jnp.maximum(m_all, group_max[(h, jj)])
                acc = sum(jnp.exp2(group_max[(h, jj)] - m_all) * parts.pop((h, jj)) for jj in range(n_groups))
                outs[h] = acc[:, :LANES] * vscale_ref[h // 2] / acc[:, LANES:LANES + 1]
    lo = _lane_iota(MLA_Q_TILE) < HALF_LANES
    for hp in range(HEADS // 2):
        o_ref[0, :, LANES * hp:LANES * (hp + 1)] = jnp.where(lo, outs[2 * hp], outs[2 * hp + 1]).astype(_BF)


def _mla_attn(q, k, v, k_meta, v_meta):
    b = q.shape[0]
    once = pl.Buffered(1)
    whole = lambda shape: pl.BlockSpec(shape, lambda bi, i: (0,) * len(shape), pipeline_mode=once)
    return pl.pallas_call(
        _mla_kernel,
        grid=(b, SEQ // MLA_Q_TILE),
        in_specs=[pl.BlockSpec((1, MLA_Q_TILE, HEADS * LANES), lambda bi, i: (bi, i, 0)),
                  pl.BlockSpec((1, SEQ, HEADS * LANES), lambda bi, i: (bi, 0, 0), pipeline_mode=once),
                  pl.BlockSpec((1, SEQ, HALF_WIDTH), lambda bi, i: (bi, 0, 0), pipeline_mode=once),
                  whole((N_META, HEADS * LANES)), whole((N_META, HALF_WIDTH))],
        out_specs=pl.BlockSpec((1, MLA_Q_TILE, HALF_WIDTH), lambda bi, i: (bi, i, 0)),
        out_shape=jax.ShapeDtypeStruct((b, SEQ, HALF_WIDTH), _BF),
        scratch_shapes=[pltpu.VMEM((HEADS // 2, SEQ, 2 * LANES), _F8),
                        pltpu.VMEM((HEADS // 2, N_META, 2 * LANES), _BF),
                        pltpu.VMEM((HEADS // 2, 1, LANES), _F32)],
        compiler_params=pltpu.CompilerParams(dimension_semantics=("arbitrary",) * 2,
                                             vmem_limit_bytes=VMEM_MIB["mla_attn"] * MIB),
        name="mla_attn",
    )(q, k, v, k_meta, v_meta)


def _na_slab_row(g):
    return int(np.clip(g * NA_GROUP_ROWS - NA_MAX_ROWS // 2, 0, NA_ROWS - NA_WIN_ROWS))


def _na_kernel(q_ref, k_ref, v_ref, km_ref, vm_ref, tab_ref, o_ref, ks_ref, vs_ref):
    @pl.when(pl.program_id(1) == 0)
    def _():
        for buf in range(NA_GROUPS):
            ks_ref[buf, NA_WIN:, :] = jnp.zeros((NA_KEYS - NA_WIN, LANES), _BF)
            vs_ref[buf, NA_WIN:, :] = jnp.zeros((NA_KEYS - NA_WIN, 2 * LANES), _BF)
            ks_ref[buf, NA_WIN:NA_WIN + N_META, :] = km_ref[...]
            vs_ref[buf, NA_WIN:NA_WIN + N_META, :LANES] = vm_ref[...]
            vs_ref[buf, :NA_WIN + N_META, LANES:] = jnp.ones((NA_WIN + N_META, LANES), _BF)

    lo = _lane_iota(NA_GROUP) < HALF_LANES

    scores, outs = {}, {}
    items = 2 * NA_GROUPS
    for n in range(items + 1):
        if n < items:
            g, hh = divmod(n, 2)
            if hh == 0:
                rows = slice(_na_slab_row(g) * GRID_W, _na_slab_row(g) * GRID_W + NA_WIN)
                ks_ref[g, :NA_WIN, :] = k_ref[0, rows, :]
                vs_ref[g, :NA_WIN, :LANES] = v_ref[0, rows, :]
                q_pair = q_ref[0, NA_GROUP * g:NA_GROUP * (g + 1), :]
            q = jnp.where(lo == (hh == 0), q_pair, jnp.zeros_like(q_pair))
            variant = 0 if g == 0 else (2 if g == NA_GROUPS - 1 else 1)
            scores[n] = _dot_nt(q, ks_ref[g]) + tab_ref[hh, variant]
        if n >= 1:
            g, hh = divmod(n - 1, 2)
            s = scores.pop(n - 1)
            m = jnp.max(s, axis=-1, keepdims=True)
            acc = _dot(jnp.exp2(s - m).astype(_BF), vs_ref[g])
            outs[hh] = acc[:, :LANES] / acc[:, LANES:LANES + 1]
            if hh == 1:
                o_ref[0, NA_GROUP * g:NA_GROUP * (g + 1), :] = jnp.where(lo, outs[0], outs[1]).astype(_BF)


def _na_attn(q, k, v, k_meta, v_meta, table):
    b = q.shape[0]
    seq_spec = pl.BlockSpec((1, SEQ, LANES), lambda hp, bi: (bi, 0, hp))
    meta_spec = pl.BlockSpec((N_META, LANES), lambda hp, bi: (0, hp))
    return pl.pallas_call(
        _na_kernel,
        grid=(HEADS // 2, b),
        in_specs=[seq_spec, seq_spec, seq_spec, meta_spec, meta_spec,
                  pl.BlockSpec((2, 3, NA_GROUP, NA_KEYS), lambda hp, bi: (hp, 0, 0, 0))],
        out_specs=seq_spec,
        out_shape=jax.ShapeDtypeStruct((b, SEQ, HALF_WIDTH), _BF),
        scratch_shapes=[pltpu.VMEM((NA_GROUPS, NA_KEYS, LANES), _BF),
                        pltpu.VMEM((NA_GROUPS, NA_KEYS, 2 * LANES), _BF)],
        compiler_params=pltpu.CompilerParams(dimension_semantics=("arbitrary",) * 2,
                                             vmem_limit_bytes=VMEM_MIB["na_attn"] * MIB),
        name="na_attn",
    )(q, k, v, k_meta, v_meta, table)


def _na_bias_table(rel_bias):
    n_dr, n_dc = 2 * NA_MAX_ROWS - 1, 2 * NA_KW - 1
    c = np.arange(GRID_W)
    cs = np.clip(c - NA_KW // 2, 0, GRID_W - NA_KW)
    col_ok = (c[None, :] >= cs[:, None]) & (c[None, :] < cs[:, None] + NA_KW)
    dc = c[None, :] - c[:, None] + NA_KW - 1
    col_pick = (dc[None] == np.arange(n_dc)[:, None, None]) & col_ok[None]
    row_pick = np.full((3, NA_GROUP_ROWS, NA_WIN_ROWS), -1)
    for v, g in enumerate((0, 1, NA_GROUPS - 1)):
        for i in range(NA_GROUP_ROWS):
            r = g * NA_GROUP_ROWS + i
            rs = int(np.clip(r - NA_MAX_ROWS // 2, 0, NA_ROWS - NA_MAX_ROWS))
            for j in range(NA_WIN_ROWS):
                kr = _na_slab_row(g) + j
                if rs <= kr < rs + NA_MAX_ROWS:
                    row_pick[v, i, j] = kr - r + NA_MAX_ROWS - 1
    row_pick = row_pick.reshape(-1)
    row_onehot = (row_pick[:, None] == np.arange(n_dr)[None, :]).astype(np.float32)
    exact = lax.Precision.HIGHEST
    lines = jnp.einsum("pr,hrd->hpd", jnp.asarray(row_onehot), rel_bias * LOG2E, precision=exact)
    blocks = jnp.einsum("hpd,dcx->hpcx", lines, jnp.asarray(col_pick.astype(np.float32)), precision=exact)
    inside = (row_pick >= 0)[:, None, None] & col_ok[None]
    blocks = jnp.where(inside[None], blocks, MASKED)
    win = blocks.reshape(HEADS, 3, NA_GROUP_ROWS, NA_WIN_ROWS, GRID_W, GRID_W)
    win = win.transpose(0, 1, 2, 4, 3, 5).reshape(HEADS, 3, NA_GROUP, NA_WIN)
    tail = jnp.concatenate([jnp.zeros((N_META,), _F32), jnp.full((NA_KEYS - NA_WIN - N_META,), MASKED, _F32)])
    return jnp.concatenate([win, jnp.broadcast_to(tail, (HEADS, 3, NA_GROUP, tail.shape[0]))], axis=-1)


def _merge_ffn2_kernel(h1_ref, oa_ref, ob_ref, gmix_ref, wga_ref, wgb_ref, wba_ref, wbb_ref, wout_ref,
                       g2_ref, wg_ref, wu_ref, wd_ref, out_ref, act_ref):
    h1 = h1_ref[...]
    xn = _rms(h1, gmix_ref[...]).astype(_BF)
    merged = (_sigmoid(_dot(xn, wga_ref[...])) * _dot(oa_ref[...], wba_ref[...])
              + _sigmoid(_dot(xn, wgb_ref[...])) * _dot(ob_ref[...], wbb_ref[...]))
    h2 = h1 + _dot(merged.astype(_BF), wout_ref[...])
    _swiglu_hidden(_rms(h2, g2_ref[...]).astype(_BF), wg_ref, wu_ref, act_ref)
    out_ref[...] = h2 + 0.5 * _swiglu_down(act_ref, wd_ref)


def _merge_ffn2(h1, oa, ob, p):
    n = h1.shape[0]
    row_spec = lambda w: pl.BlockSpec((TOKEN_TILE, w), lambda i: (i, 0))
    consts = [p["gmix"], p["wga"], p["wgb"], p["wba"], p["wbb"], p["wout"], p["g2"], p["wg2"], p["wu2"], p["wd2"]]
    return pl.pallas_call(
        _merge_ffn2_kernel,
        grid=(n // TOKEN_TILE,),
        in_specs=[row_spec(D_MODEL), row_spec(HALF_WIDTH), row_spec(HALF_WIDTH)]
                 + [_const_spec(c.shape) for c in consts],
        out_specs=row_spec(D_MODEL),
        out_shape=jax.ShapeDtypeStruct((n, D_MODEL), _F32),
        scratch_shapes=[pltpu.VMEM((N_FF_CHUNKS, TOKEN_TILE, FF_CHUNK), _BF)],
        compiler_params=pltpu.CompilerParams(dimension_semantics=("arbitrary",),
                                             vmem_limit_bytes=VMEM_MIB["merge_ffn2"] * MIB),
        name="merge_ffn2",
    )(h1, oa, ob, *consts)


def _rope_swapped(w):
    half = MLA_ROPE // 2
    return jnp.concatenate([w, w[..., half:], w[..., :half]], axis=-1)


def _lane_row(v, offset):
    return jnp.zeros((1, LANES), _F32).at[0, offset:offset + v.shape[0]].set(v)


def _ff_weights(w_gate, w_up, w_down):
    return w_gate.astype(_BF), w_up.astype(_BF), w_down.reshape(N_FF_CHUNKS, FF_CHUNK, D_MODEL).astype(_BF)


def _rope_tables():
    half = MLA_ROPE // 2
    pos = jnp.arange(N_META + SEQ, dtype=_F32)
    inv = ROPE_THETA ** (-jnp.arange(half, dtype=_F32) / half)
    ang = pos[:, None] * inv[None, :]
    cos, sin = jnp.cos(ang), jnp.sin(ang)
    n = pos.shape[0]
    tail = jnp.zeros((n, LANES - MLA_QK), _F32)
    c = jnp.concatenate([jnp.ones((n, MLA_NOPE), _F32), cos, cos, tail], axis=1)
    s = jnp.concatenate([jnp.zeros((n, MLA_NOPE), _F32), -sin, sin, tail], axis=1)
    return c, s


def kernel(x, meta_tokens, ffn1_norm, ffn1_w_gate, ffn1_w_up, ffn1_w_down, mix_norm, w_in, mla_q_a_norm, mla_w_uq,
           mla_kv_a_norm, mla_w_ukv, mla_q_head_norm, mla_k_head_norm, na_q_head_norm, na_k_head_norm, na_rel_bias,
           w_branch_a, w_branch_b, w_out, ffn2_norm, ffn2_w_gate, ffn2_w_up, ffn2_w_down):
    assert ffn1_norm.shape[0] == 1, "single-layer block"
    b = x.shape[0]
    row = lambda v: v.reshape(1, -1).astype(_F32)
    splits = np.cumsum((Q_LORA, KV_LORA, MLA_ROPE, HALF_WIDTH, HALF_WIDTH, HALF_WIDTH, D_MODEL, D_MODEL))
    w_cq, w_ckv, w_kr, w_qna, w_kna, w_vna, w_ga, w_gb = jnp.split(w_in[0], splits[:-1].tolist(), axis=1)
    w_kr_blk = jnp.pad(_rope_swapped(w_kr), ((0, 0), (MLA_NOPE, 0)))
    uq = mla_w_uq[0].reshape(Q_LORA, HEADS, MLA_QK)
    uq = jnp.concatenate([uq[..., :MLA_NOPE], _rope_swapped(uq[..., MLA_NOPE:])], axis=-1).reshape(Q_LORA, -1)
    kv_up = mla_w_ukv[0].reshape(KV_LORA, HEADS, 2 * MLA_NOPE)
    k_nope_up = jnp.pad(kv_up[..., :MLA_NOPE], ((0, 0), (0, 0), (0, LANES - MLA_NOPE))).reshape(KV_LORA, -1)
    gq_mla = mla_q_head_norm[0] * (MLA_QK ** -0.5 * LOG2E)
    gk_mla = mla_k_head_norm[0]
    gq_na = na_q_head_norm[0] * (NA_HEAD_DIM ** -0.5 * LOG2E)
    p = {
        "g1": row(ffn1_norm[0]), "gmix": row(mix_norm[0]), "g2": row(ffn2_norm[0]),
        "win_r": jnp.concatenate([w_cq, w_ckv, w_kr_blk, w_qna, w_kna, w_vna], axis=1).astype(_BF),
        "qag": row(mla_q_a_norm[0]), "kvag": row(mla_kv_a_norm[0]),
        "wuq": uq.astype(_BF),
        "wukv": jnp.concatenate([k_nope_up, kv_up[..., MLA_NOPE:].reshape(KV_LORA, -1)], axis=1).astype(_BF),
        "gq": row(jnp.concatenate([gq_mla[:MLA_NOPE], _rope_swapped(gq_mla[MLA_NOPE:])])),
        "gkn": _lane_row(gk_mla[:MLA_NOPE], 0),
        "gkr": _lane_row(_rope_swapped(gk_mla[MLA_NOPE:]), MLA_NOPE),
        "gqna": row(jnp.concatenate([gq_na, gq_na])),
        "gkna": row(jnp.concatenate([na_k_head_norm[0], na_k_head_norm[0]])),
        "wga": w_ga.astype(_BF), "wgb": w_gb.astype(_BF),
        "wba": w_branch_a[0].astype(_BF), "wbb": w_branch_b[0].astype(_BF), "wout": w_out[0].astype(_BF),
    }
    p["wg1"], p["wu1"], p["wd1"] = _ff_weights(ffn1_w_gate[0], ffn1_w_up[0], ffn1_w_down[0])
    p["wg2"], p["wu2"], p["wd2"] = _ff_weights(ffn2_w_gate[0], ffn2_w_up[0], ffn2_w_down[0])
    assert p["win_r"].shape[1] == _IN_R_WIDTH and p["wuq"].shape[1] == HEADS * LANES

    tabs = _rope_tables()
    meta = _ffn1_proj(meta_tokens.astype(_F32), [t[:N_META] for t in tabs], p, N_META, VMEM_MIB["ffn1_proj_meta"])
    _, _, km_meta, vm_meta, _, kn_meta, vn_meta = meta
    real = _ffn1_proj(x.reshape(b * SEQ, D_MODEL), [t[N_META:] for t in tabs], p, TOKEN_TILE, VMEM_MIB["ffn1_proj"])
    h1, qm, km, vm, qn, kn, vn = real
    seq = lambda a: a.reshape(b, SEQ, a.shape[-1])
    o_a = _mla_attn(seq(qm), seq(km), seq(vm), km_meta, vm_meta)
    o_b = _na_attn(seq(qn), seq(kn), seq(vn), kn_meta, vn_meta, _na_bias_table(na_rel_bias[0]))
    out = _merge_ffn2(h1, o_a.reshape(b * SEQ, HALF_WIDTH), o_b.reshape(b * SEQ, HALF_WIDTH), p)
    return out.reshape(b, SEQ, D_MODEL)
```

```python
import math

import jax
import jax.numpy as jnp
import numpy as np
from jax import lax
from jax.experimental import pallas as pl
from jax.experimental.pallas import tpu as pltpu

D_MODEL = 1024
SEQ = 4096
N_META = 16
GRID_W = 64
HEADS = 8
MLA_NOPE = 64
MLA_ROPE = 32
MLA_QK = MLA_NOPE + MLA_ROPE
Q_LORA = 384
KV_LORA = 256
NA_HEAD_DIM = 64
NA_MAX_ROWS = 8
NA_KW = 16
D_FF = 2816
ROPE_THETA = 10000.0
EPS = 1e-6
HALF_WIDTH = 512
LOG2E = math.log2(math.e)

LANES = 128
HALF_LANES = LANES // 2
FF_CHUNK = 256
N_FF_CHUNKS = D_FF // FF_CHUNK
assert N_FF_CHUNKS * FF_CHUNK == D_FF

TOKEN_TILE = 512
MLA_Q_TILE = 256
MLA_K_TILE = 256
MLA_GROUP_TILES = 8
MLA_PV_LAG = 12
MLA_P_SHIFT = 7.0
MLA_V_TARGET = 256.0
NA_GROUP_ROWS = 4
NA_GROUP = NA_GROUP_ROWS * GRID_W
NA_WIN_ROWS = NA_GROUP_ROWS + NA_MAX_ROWS - 1
NA_WIN = NA_WIN_ROWS * GRID_W
NA_KEYS = 768
assert NA_WIN + N_META <= NA_KEYS and NA_KEYS % 256 == 0
NA_ROWS = SEQ // GRID_W
NA_GROUPS = NA_ROWS // NA_GROUP_ROWS
MASKED = -1e30

MIB = 2 ** 20
V7X_VMEM_MIB = 64
VMEM_MIB = {"ffn1_proj": 56, "ffn1_proj_meta": 40, "mla_attn": 56, "na_attn": 48, "merge_ffn2": 56}
assert max(VMEM_MIB.values()) < V7X_VMEM_MIB

_O_CQ, _O_CKV, _O_KR = 0, Q_LORA, Q_LORA + KV_LORA
_O_QNA = _O_KR + LANES
_O_KNA = _O_QNA + HALF_WIDTH
_O_VNA = _O_KNA + HALF_WIDTH
_IN_R_WIDTH = _O_VNA + HALF_WIDTH

_BF = jnp.bfloat16
_F32 = jnp.float32
_F8 = jnp.float8_e4m3fn


def _dot(a, b):
    return jnp.dot(a, b, preferred_element_type=_F32)


def _dot_nt(a, b):
    return lax.dot_general(a, b, (((1,), (1,)), ((), ())), preferred_element_type=_F32)


def _sigmoid(x):
    return 1.0 / (1.0 + jnp.exp(-x))


def _rms(x, g):
    return x * lax.rsqrt(jnp.mean(x * x, axis=-1, keepdims=True) + EPS) * g


def _lane_iota(rows):
    return lax.broadcasted_iota(jnp.int32, (rows, LANES), 1)


def _swiglu_hidden(xn, wg_ref, wu_ref, act_ref, side_work=()):
    side_work = list(side_work)
    per_chunk = -(-len(side_work) // N_FF_CHUNKS)
    for c in range(N_FF_CHUNKS):
        cols = slice(FF_CHUNK * c, FF_CHUNK * (c + 1))
        g = _dot(xn, wg_ref[:, cols])
        u = _dot(xn, wu_ref[:, cols])
        act_ref[c] = (g * _sigmoid(g) * u).astype(_BF)
        for piece in side_work[c * per_chunk:(c + 1) * per_chunk]:
            piece()


def _swiglu_down(act_ref, wd_ref):
    y = _dot(act_ref[0], wd_ref[0])
    for c in range(1, N_FF_CHUNKS):
        y += _dot(act_ref[c], wd_ref[c])
    return y


def _rope(x, c, s):
    return x * c + pltpu.roll(x, LANES - MLA_ROPE, 1) * s


def _half_head_norm(blk, g, lo):
    sq = blk * blk
    s_lo = jnp.sum(jnp.where(lo, sq, 0.0), axis=-1, keepdims=True)
    s_hi = jnp.sum(jnp.where(lo, 0.0, sq), axis=-1, keepdims=True)
    r = lax.rsqrt(jnp.where(lo, s_lo, s_hi) * (1.0 / NA_HEAD_DIM) + EPS)
    return (blk * r * g).astype(_BF)


def _ffn1_proj_kernel(x_ref, g1_ref, wg_ref, wu_ref, wd_ref, gmix_ref, win_ref, qag_ref, wuq_ref,
                      kvag_ref, wukv_ref, gq_ref, gkn_ref, gkr_ref, gqna_ref, gkna_ref, rc_ref, rs_ref,
                      h1_ref, qm_ref, km_ref, vm_ref, qn_ref, kn_ref, vn_ref,
                      act_ref, raw_q_ref, raw_k_ref, raw_kr_ref, raw_qn_ref, raw_kn_ref):
    step = pl.program_id(0)
    last = pl.num_programs(0) - 1
    rows = x_ref.shape[0]
    raw_refs = (raw_q_ref, raw_k_ref, raw_kr_ref, raw_qn_ref, raw_kn_ref)

    def head_work():
        qk_lanes = _lane_iota(rows) < MLA_QK
        lo = _lane_iota(rows) < HALF_LANES
        shared = {}

        def q_head(h):
            blk = slice(LANES * h, LANES * (h + 1))
            qh = raw_q_ref[:, blk]
            ss = jnp.sum(jnp.where(qk_lanes, qh * qh, 0.0), axis=-1, keepdims=True)
            r = lax.rsqrt(ss * (1.0 / MLA_QK) + EPS)
            qm_ref[:, blk] = _rope(qh * r * gq_ref[...], rc_ref[...], rs_ref[...]).astype(_BF)

        def k_rope():
            kr = raw_kr_ref[...]
            shared["ss_r"] = jnp.sum(jnp.where(qk_lanes, kr * kr, 0.0), axis=-1, keepdims=True)
            shared["krr"] = _rope(kr * gkr_ref[...], rc_ref[...], rs_ref[...])

        def k_head(h):
            blk = slice(LANES * h, LANES * (h + 1))
            kh = raw_k_ref[:, blk]
            ss = jnp.sum(kh * kh, axis=-1, keepdims=True) + shared["ss_r"]
            r = lax.rsqrt(ss * (1.0 / MLA_QK) + EPS)
            km_ref[:, blk] = ((kh * gkn_ref[...] + shared["krr"]) * r).astype(_BF)

        def na_block(raw_ref, g_ref, out_ref, j):
            blk = slice(LANES * j, LANES * (j + 1))
            out_ref[:, blk] = _half_head_norm(raw_ref[:, blk], g_ref[...], lo)

        pieces = [lambda h=h: q_head(h) for h in range(HEADS)] + [k_rope]
        pieces += [lambda h=h: k_head(h) for h in range(HEADS)]
        for raw_ref, g_ref, out_ref in ((raw_qn_ref, gqna_ref, qn_ref), (raw_kn_ref, gkna_ref, kn_ref)):
            pieces += [lambda a=(raw_ref, g_ref, out_ref, j): na_block(*a) for j in range(HALF_WIDTH // LANES)]
        return pieces

    @pl.when(step == 0)
    def _():
        for ref in raw_refs:
            ref[...] = jnp.zeros_like(ref)

    @pl.when(step < last)
    def _():
        xn = _rms(x_ref[...], g1_ref[...]).astype(_BF)
        _swiglu_hidden(xn, wg_ref, wu_ref, act_ref, side_work=head_work())

    @pl.when(step + 1 <= last)
    def _():
        h1 = x_ref[...] + 0.5 * _swiglu_down(act_ref, wd_ref)
        h1_ref[...] = h1
        xn = _rms(h1, gmix_ref[...]).astype(_BF)
        cq = _dot(xn, win_ref[:, _O_CQ:_O_CQ + Q_LORA])
        raw_q_ref[...] = _dot(_rms(cq, qag_ref[...]).astype(_BF), wuq_ref[...])
        ckv = _dot(xn, win_ref[:, _O_CKV:_O_CKV + KV_LORA])
        kvu = _dot(_rms(ckv, kvag_ref[...]).astype(_BF), wukv_ref[...])
        raw_k_ref[...] = kvu[:, :HEADS * LANES]
        vm_ref[...] = kvu[:, HEADS * LANES:].astype(_BF)
        raw_kr_ref[...] = _dot(xn, win_ref[:, _O_KR:_O_KR + LANES])
        raw_qn_ref[...] = _dot(xn, win_ref[:, _O_QNA:_O_QNA + HALF_WIDTH])
        raw_kn_ref[...] = _dot(xn, win_ref[:, _O_KNA:_O_KNA + HALF_WIDTH])
        vn_ref[...] = _dot(xn, win_ref[:, _O_VNA:_O_VNA + HALF_WIDTH]).astype(_BF)

    @pl.when(step == last)
    def _():
        for piece in head_work():
            piece()


def _const_spec(shape):
    nd = len(shape)
    return pl.BlockSpec(shape, lambda i: (0,) * nd, pipeline_mode=pl.Buffered(1))


def _ffn1_proj(x2d, rope_tabs, p, tile, vmem_mib):
    n_tiles = x2d.shape[0] // tile
    tabs_per_seq = rope_tabs[0].shape[0] // tile
    cur = lambda i: jnp.minimum(i, n_tiles - 1)
    prev = lambda i: jnp.maximum(i - 1, 0)
    cur_spec = lambda w: pl.BlockSpec((tile, w), lambda i: (cur(i), 0))
    prev_spec = lambda w: pl.BlockSpec((tile, w), lambda i: (prev(i), 0))
    tab_spec = pl.BlockSpec((tile, LANES), lambda i: (prev(i) % tabs_per_seq, 0))
    consts = [p["g1"], p["wg1"], p["wu1"], p["wd1"], p["gmix"], p["win_r"], p["qag"], p["wuq"],
              p["kvag"], p["wukv"], p["gq"], p["gkn"], p["gkr"], p["gqna"], p["gkna"]]
    wide = HEADS * LANES
    widths = [wide, wide, HALF_WIDTH, HALF_WIDTH, HALF_WIDTH, HALF_WIDTH]
    out_specs = [cur_spec(D_MODEL), prev_spec(wide), prev_spec(wide), cur_spec(HALF_WIDTH),
                 prev_spec(HALF_WIDTH), prev_spec(HALF_WIDTH), cur_spec(HALF_WIDTH)]
    out_shape = [jax.ShapeDtypeStruct((x2d.shape[0], D_MODEL), _F32)]
    out_shape += [jax.ShapeDtypeStruct((x2d.shape[0], w), _BF) for w in widths]
    raw = lambda w: pltpu.VMEM((tile, w), _F32)
    return pl.pallas_call(
        _ffn1_proj_kernel,
        grid=(n_tiles + 1,),
        in_specs=[cur_spec(D_MODEL)] + [_const_spec(c.shape) for c in consts] + [tab_spec] * 2,
        out_specs=out_specs,
        out_shape=out_shape,
        scratch_shapes=[pltpu.VMEM((N_FF_CHUNKS, tile, FF_CHUNK), _BF),
                        raw(wide), raw(wide), raw(LANES), raw(HALF_WIDTH), raw(HALF_WIDTH)],
        compiler_params=pltpu.CompilerParams(dimension_semantics=("arbitrary",),
                                             vmem_limit_bytes=vmem_mib * MIB),
        name="ffn1_proj",
    )(x2d, *consts, *rope_tabs)


def _mla_kernel(q_ref, k_ref, v_ref, km_ref, vm_ref, o_ref, vext_ref, vmext_ref, vscale_ref):
    @pl.when(pl.program_id(1) == 0)
    def _():
        lo_row = lax.broadcasted_iota(jnp.int32, (1, LANES), 1) < HALF_LANES
        for hp in range(HEADS // 2):
            blk = slice(LANES * hp, LANES * (hp + 1))
            v_pair, vm_pair = v_ref[0, :, blk].astype(_F32), vm_ref[:, blk].astype(_F32)
            amax = jnp.maximum(jnp.max(jnp.abs(v_pair), axis=0, keepdims=True),
                               jnp.max(jnp.abs(vm_pair), axis=0, keepdims=True))
            head_max = jnp.where(lo_row, jnp.max(jnp.where(lo_row, amax, 0.0), axis=1, keepdims=True),
                                 jnp.max(jnp.where(lo_row, 0.0, amax), axis=1, keepdims=True))
            scale = jnp.maximum(head_max, 1e-30) * (1.0 / MLA_V_TARGET)
            vscale_ref[hp] = scale
            vext_ref[hp, :, :LANES] = (v_pair / scale).astype(_F8)
            vext_ref[hp, :, LANES:] = jnp.ones((SEQ, LANES), _F8)
            vmext_ref[hp, :, :LANES] = (vm_pair / scale).astype(_BF)
            vmext_ref[hp, :, LANES:] = jnp.ones((N_META, LANES), _BF)

    n_tiles = SEQ // MLA_K_TILE
    n_groups = n_tiles // MLA_GROUP_TILES
    slots = HEADS * n_tiles
    scores, meta_scores, group_max, parts, outs = {}, {}, {}, {}, {}
    run_max = None
    for g in range(slots + MLA_PV_LAG):
        if g < slots:
            h, t = divmod(g, n_tiles)
            sl = slice(LANES * h, LANES * (h + 1))
            if t == 0:
                q = q_ref[0, :, sl]
                meta_scores[h] = _dot_nt(q, km_ref[:, sl])
            s = _dot_nt(q, k_ref[0, MLA_K_TILE * t:MLA_K_TILE * (t + 1), sl])
            run_max = s if t % MLA_GROUP_TILES == 0 else jnp.maximum(run_max, s)
            scores[(h, t)] = s
            if t % MLA_GROUP_TILES == MLA_GROUP_TILES - 1:
                m = jnp.max(run_max, axis=-1, keepdims=True)
                if t // MLA_GROUP_TILES == 0:
                    m = jnp.maximum(m, jnp.max(meta_scores[h], axis=-1, keepdims=True))
                group_max[(h, t // MLA_GROUP_TILES)] = m - MLA_P_SHIFT
        gp = g - MLA_PV_LAG
        if gp >= 0:
            h, t = divmod(gp, n_tiles)
            j = t // MLA_GROUP_TILES
            m = group_max[(h, j)]
            part = _dot(jnp.exp2(scores.pop((h, t)) - m).astype(_F8),
                        vext_ref[h // 2, MLA_K_TILE * t:MLA_K_TILE * (t + 1), :])
            if t == 0:
                part += _dot(jnp.exp2(meta_scores[h] - m).astype(_BF), vmext_ref[h // 2])
            parts[(h, j)] = part if t % MLA_GROUP_TILES == 0 else parts[(h, j)] + part
            if t == n_tiles - 1:
                m_all = group_max[(h, 0)]
                for jj in range(1, n_groups):
                    m_all = jnp.maximum(m_all, group_max[(h, jj)])
                acc = sum(jnp.exp2(group_max[(h, jj)] - m_all) * parts.pop((h, jj)) for jj in range(n_groups))
                outs[h] = acc[:, :LANES] * vscale_ref[h // 2] / acc[:, LANES:LANES + 1]
    lo = _lane_iota(MLA_Q_TILE) < HALF_LANES
    for hp in range(HEADS // 2):
        o_ref[0, :, LANES * hp:LANES * (hp + 1)] = jnp.where(lo, outs[2 * hp], outs[2 * hp + 1]).astype(_BF)


def _mla_attn(q, k, v, k_meta, v_meta):
    b = q.shape[0]
    once = pl.Buffered(1)
    whole = lambda shape: pl.BlockSpec(shape, lambda bi, i: (0,) * len(shape), pipeline_mode=once)
    return pl.pallas_call(
        _mla_kernel,
        grid=(b, SEQ // MLA_Q_TILE),
        in_specs=[pl.BlockSpec((1, MLA_Q_TILE, HEADS * LANES), lambda bi, i: (bi, i, 0)),
                  pl.BlockSpec((1, SEQ, HEADS * LANES), lambda bi, i: (bi, 0, 0), pipeline_mode=once),
                  pl.BlockSpec((1, SEQ, HALF_WIDTH), lambda bi, i: (bi, 0, 0), pipeline_mode=once),
                  whole((N_META, HEADS * LANES)), whole((N_META, HALF_WIDTH))],
        out_specs=pl.BlockSpec((1, MLA_Q_TILE, HALF_WIDTH), lambda bi, i: (bi, i, 0)),
        out_shape=jax.ShapeDtypeStruct((b, SEQ, HALF_WIDTH), _BF),
        scratch_shapes=[pltpu.VMEM((HEADS // 2, SEQ, 2 * LANES), _F8),
                        pltpu.VMEM((HEADS // 2, N_META, 2 * LANES), _BF),
                        pltpu.VMEM((HEADS // 2, 1, LANES), _F32)],
        compiler_params=pltpu.CompilerParams(dimension_semantics=("arbitrary",) * 2,
                                             vmem_limit_bytes=VMEM_MIB["mla_attn"] * MIB),
        name="mla_attn",
    )(q, k, v, k_meta, v_meta)


def _na_slab_row(g):
    return int(np.clip(g * NA_GROUP_ROWS - NA_MAX_ROWS // 2, 0, NA_ROWS - NA_WIN_ROWS))


def _na_kernel(q_ref, k_ref, v_ref, km_ref, vm_ref, tab_ref, o_ref, ks_ref, vs_ref):
    @pl.when(pl.program_id(1) == 0)
    def _():
        for buf in range(NA_GROUPS):
            ks_ref[buf, NA_WIN:, :] = jnp.zeros((NA_KEYS - NA_WIN, LANES), _BF)
            vs_ref[buf, NA_WIN:, :] = jnp.zeros((NA_KEYS - NA_WIN, 2 * LANES), _BF)
            ks_ref[buf, NA_WIN:NA_WIN + N_META, :] = km_ref[...]
            vs_ref[buf, NA_WIN:NA_WIN + N_META, :LANES] = vm_ref[...]
            vs_ref[buf, :NA_WIN + N_META, LANES:] = jnp.ones((NA_WIN + N_META, LANES), _BF)

    lo = _lane_iota(NA_GROUP) < HALF_LANES

    scores, outs = {}, {}
    items = 2 * NA_GROUPS
    for n in range(items + 1):
        if n < items:
            g, hh = divmod(n, 2)
            if hh == 0:
                rows = slice(_na_slab_row(g) * GRID_W, _na_slab_row(g) * GRID_W + NA_WIN)
                ks_ref[g, :NA_WIN, :] = k_ref[0, rows, :]
                vs_ref[g, :NA_WIN, :LANES] = v_ref[0, rows, :]
                q_pair = q_ref[0, NA_GROUP * g:NA_GROUP * (g + 1), :]
            q = jnp.where(lo == (hh == 0), q_pair, jnp.zeros_like(q_pair))
            variant = 0 if g == 0 else (2 if g == NA_GROUPS - 1 else 1)
            scores[n] = _dot_nt(q, ks_ref[g]) + tab_ref[hh, variant]
        if n >= 1:
            g, hh = divmod(n - 1, 2)
            s = scores.pop(n - 1)
            m = jnp.max(s, axis=-1, keepdims=True)
            acc = _dot(jnp.exp2(s - m).astype(_BF), vs_ref[g])
            outs[hh] = acc[:, :LANES] / acc[:, LANES:LANES + 1]
            if hh == 1:
                o_ref[0, NA_GROUP * g:NA_GROUP * (g + 1), :] = jnp.where(lo, outs[0], outs[1]).astype(_BF)


def _na_attn(q, k, v, k_meta, v_meta, table):
    b = q.shape[0]
    seq_spec = pl.BlockSpec((1, SEQ, LANES), lambda hp, bi: (bi, 0, hp))
    meta_spec = pl.BlockSpec((N_META, LANES), lambda hp, bi: (0, hp))
    return pl.pallas_call(
        _na_kernel,
        grid=(HEADS // 2, b),
        in_specs=[seq_spec, seq_spec, seq_spec, meta_spec, meta_spec,
                  pl.BlockSpec((2, 3, NA_GROUP, NA_KEYS), lambda hp, bi: (hp, 0, 0, 0))],
        out_specs=seq_spec,
        out_shape=jax.ShapeDtypeStruct((b, SEQ, HALF_WIDTH), _BF),
        scratch_shapes=[pltpu.VMEM((NA_GROUPS, NA_KEYS, LANES), _BF),
                        pltpu.VMEM((NA_GROUPS, NA_KEYS, 2 * LANES), _BF)],
        compiler_params=pltpu.CompilerParams(dimension_semantics=("arbitrary",) * 2,
                                             vmem_limit_bytes=VMEM_MIB["na_attn"] * MIB),
        name="na_attn",
    )(q, k, v, k_meta, v_meta, table)


def _na_bias_table(rel_bias):
    n_dr, n_dc = 2 * NA_MAX_ROWS - 1, 2 * NA_KW - 1
    c = np.arange(GRID_W)
    cs = np.clip(c - NA_KW // 2, 0, GRID_W - NA_KW)
    col_ok = (c[None, :] >= cs[:, None]) & (c[None, :] < cs[:, None] + NA_KW)
    dc = c[None, :] - c[:, None] + NA_KW - 1
    col_pick = (dc[None] == np.arange(n_dc)[:, None, None]) & col_ok[None]
    row_pick = np.full((3, NA_GROUP_ROWS, NA_WIN_ROWS), -1)
    for v, g in enumerate((0, 1, NA_GROUPS - 1)):
        for i in range(NA_GROUP_ROWS):
            r = g * NA_GROUP_ROWS + i
            rs = int(np.clip(r - NA_MAX_ROWS // 2, 0, NA_ROWS - NA_MAX_ROWS))
            for j in range(NA_WIN_ROWS):
                kr = _na_slab_row(g) + j
                if rs <= kr < rs + NA_MAX_ROWS:
                    row_pick[v, i, j] = kr - r + NA_MAX_ROWS - 1
    row_pick = row_pick.reshape(-1)
    row_onehot = (row_pick[:, None] == np.arange(n_dr)[None, :]).astype(np.float32)
    exact = lax.Precision.HIGHEST
    lines = jnp.einsum("pr,hrd->hpd", jnp.asarray(row_onehot), rel_bias * LOG2E, precision=exact)
    blocks = jnp.einsum("hpd,dcx->hpcx", lines, jnp.asarray(col_pick.astype(np.float32)), precision=exact)
    inside = (row_pick >= 0)[:, None, None] & col_ok[None]
    blocks = jnp.where(inside[None], blocks, MASKED)
    win = blocks.reshape(HEADS, 3, NA_GROUP_ROWS, NA_WIN_ROWS, GRID_W, GRID_W)
    win = win.transpose(0, 1, 2, 4, 3, 5).reshape(HEADS, 3, NA_GROUP, NA_WIN)
    tail = jnp.concatenate([jnp.zeros((N_META,), _F32), jnp.full((NA_KEYS - NA_WIN - N_META,), MASKED, _F32)])
    return jnp.concatenate([win, jnp.broadcast_to(tail, (HEADS, 3, NA_GROUP, tail.shape[0]))], axis=-1)


def _merge_ffn2_kernel(h1_ref, oa_ref, ob_ref, gmix_ref, wga_ref, wgb_ref, wba_ref, wbb_ref, wout_ref,
                       g2_ref, wg_ref, wu_ref, wd_ref, out_ref, act_ref):
    h1 = h1_ref[...]
    xn = _rms(h1, gmix_ref[...]).astype(_BF)
    merged = (_sigmoid(_dot(xn, wga_ref[...])) * _dot(oa_ref[...], wba_ref[...])
              + _sigmoid(_dot(xn, wgb_ref[...])) * _dot(ob_ref[...], wbb_ref[...]))
    h2 = h1 + _dot(merged.astype(_BF), wout_ref[...])
    _swiglu_hidden(_rms(h2, g2_ref[...]).astype(_BF), wg_ref, wu_ref, act_ref)
    out_ref[...] = h2 + 0.5 * _swiglu_down(act_ref, wd_ref)


def _merge_ffn2(h1, oa, ob, p):
    n = h1.shape[0]
    row_spec = lambda w: pl.BlockSpec((TOKEN_TILE, w), lambda i: (i, 0))
    consts = [p["gmix"], p["wga"], p["wgb"], p["wba"], p["wbb"], p["wout"], p["g2"], p["wg2"], p["wu2"], p["wd2"]]
    return pl.pallas_call(
        _merge_ffn2_kernel,
        grid=(n // TOKEN_TILE,),
        in_specs=[row_spec(D_MODEL), row_spec(HALF_WIDTH), row_spec(HALF_WIDTH)]
                 + [_const_spec(c.shape) for c in consts],
        out_specs=row_spec(D_MODEL),
        out_shape=jax.ShapeDtypeStruct((n, D_MODEL), _F32),
        scratch_shapes=[pltpu.VMEM((N_FF_CHUNKS, TOKEN_TILE, FF_CHUNK), _BF)],
        compiler_params=pltpu.CompilerParams(dimension_semantics=("arbitrary",),
                                             vmem_limit_bytes=VMEM_MIB["merge_ffn2"] * MIB),
        name="merge_ffn2",
    )(h1, oa, ob, *consts)


def _rope_swapped(w):
    half = MLA_ROPE // 2
    return jnp.concatenate([w, w[..., half:], w[..., :half]], axis=-1)


def _lane_row(v, offset):
    return jnp.zeros((1, LANES), _F32).at[0, offset:offset + v.shape[0]].set(v)


def _ff_weights(w_gate, w_up, w_down):
    return w_gate.astype(_BF), w_up.astype(_BF), w_down.reshape(N_FF_CHUNKS, FF_CHUNK, D_MODEL).astype(_BF)


def _rope_tables():
    half = MLA_ROPE // 2
    pos = jnp.arange(N_META + SEQ, dtype=_F32)
    inv = ROPE_THETA ** (-jnp.arange(half, dtype=_F32) / half)
    ang = pos[:, None] * inv[None, :]
    cos, sin = jnp.cos(ang), jnp.sin(ang)
    n = pos.shape[0]
    tail = jnp.zeros((n, LANES - MLA_QK), _F32)
    c = jnp.concatenate([jnp.ones((n, MLA_NOPE), _F32), cos, cos, tail], axis=1)
    s = jnp.concatenate([jnp.zeros((n, MLA_NOPE), _F32), -sin, sin, tail], axis=1)
    return c, s


def kernel(x, meta_tokens, ffn1_norm, ffn1_w_gate, ffn1_w_up, ffn1_w_down, mix_norm, w_in, mla_q_a_norm, mla_w_uq,
           mla_kv_a_norm, mla_w_ukv, mla_q_head_norm, mla_k_head_norm, na_q_head_norm, na_k_head_norm, na_rel_bias,
           w_branch_a, w_branch_b, w_out, ffn2_norm, ffn2_w_gate, ffn2_w_up, ffn2_w_down):
    assert ffn1_norm.shape[0] == 1, "single-layer block"
    b = x.shape[0]
    row = lambda v: v.reshape(1, -1).astype(_F32)
    splits = np.cumsum((Q_LORA, KV_LORA, MLA_ROPE, HALF_WIDTH, HALF_WIDTH, HALF_WIDTH, D_MODEL, D_MODEL))
    w_cq, w_ckv, w_kr, w_qna, w_kna, w_vna, w_ga, w_gb = jnp.split(w_in[0], splits[:-1].tolist(), axis=1)
    w_kr_blk = jnp.pad(_rope_swapped(w_kr), ((0, 0), (MLA_NOPE, 0)))
    uq = mla_w_uq[0].reshape(Q_LORA, HEADS, MLA_QK)
    uq = jnp.concatenate([uq[..., :MLA_NOPE], _rope_swapped(uq[..., MLA_NOPE:])], axis=-1).reshape(Q_LORA, -1)
    kv_up = mla_w_ukv[0].reshape(KV_LORA, HEADS, 2 * MLA_NOPE)
    k_nope_up = jnp.pad(kv_up[..., :MLA_NOPE], ((0, 0), (0, 0), (0, LANES - MLA_NOPE))).reshape(KV_LORA, -1)
    gq_mla = mla_q_head_norm[0] * (MLA_QK ** -0.5 * LOG2E)
    gk_mla = mla_k_head_norm[0]
    gq_na = na_q_head_norm[0] * (NA_HEAD_DIM ** -0.5 * LOG2E)
    p = {
        "g1": row(ffn1_norm[0]), "gmix": row(mix_norm[0]), "g2": row(ffn2_norm[0]),
        "win_r": jnp.concatenate([w_cq, w_ckv, w_kr_blk, w_qna, w_kna, w_vna], axis=1).astype(_BF),
        "qag": row(mla_q_a_norm[0]), "kvag": row(mla_kv_a_norm[0]),
        "wuq": uq.astype(_BF),
        "wukv": jnp.concatenate([k_nope_up, kv_up[..., MLA_NOPE:].reshape(KV_LORA, -1)], axis=1).astype(_BF),
        "gq": row(jnp.concatenate([gq_mla[:MLA_NOPE], _rope_swapped(gq_mla[MLA_NOPE:])])),
        "gkn": _lane_row(gk_mla[:MLA_NOPE], 0),
        "gkr": _lane_row(_rope_swapped(gk_mla[MLA_NOPE:]), MLA_NOPE),
        "gqna": row(jnp.concatenate([gq_na, gq_na])),
        "gkna": row(jnp.concatenate([na_k_head_norm[0], na_k_head_norm[0]])),
        "wga": w_ga.astype(_BF), "wgb": w_gb.astype(_BF),
        "wba": w_branch_a[0].astype(_BF), "wbb": w_branch_b[0].astype(_BF), "wout": w_out[0].astype(_BF),
    }
    p["wg1"], p["wu1"], p["wd1"] = _ff_weights(ffn1_w_gate[0], ffn1_w_up[0], ffn1_w_down[0])
    p["wg2"], p["wu2"], p["wd2"] = _ff_weights(ffn2_w_gate[0], ffn2_w_up[0], ffn2_w_down[0])
    assert p["win_r"].shape[1] == _IN_R_WIDTH and p["wuq"].shape[1] == HEADS * LANES

    tabs = _rope_tables()
    meta = _ffn1_proj(meta_tokens.astype(_F32), [t[:N_META] for t in tabs], p, N_META, VMEM_MIB["ffn1_proj_meta"])
    _, _, km_meta, vm_meta, _, kn_meta, vn_meta = meta
    real = _ffn1_proj(x.reshape(b * SEQ, D_MODEL), [t[N_META:] for t in tabs], p, TOKEN_TILE, VMEM_MIB["ffn1_proj"])
    h1, qm, km, vm, qn, kn, vn = real
    seq = lambda a: a.reshape(b, SEQ, a.shape[-1])
    o_a = _mla_attn(seq(qm), seq(km), seq(vm), km_meta, vm_meta)
    o_b = _na_attn(seq(qn), seq(kn), seq(vn), kn_meta, vn_meta, _na_bias_table(na_rel_bias[0]))
    out = _merge_ffn2(h1, o_a.reshape(b * SEQ, HALF_WIDTH), o_b.reshape(b * SEQ, HALF_WIDTH), p)
    return out.reshape(b, SEQ, D_MODEL)
```
